```python
import math
import jax, jax.numpy as jnp
from jax import lax
import numpy as np

D_MODEL = 1024
BATCH = 8
SEQ = 2048
DEPTH = 2
DEC_BATCH = 128
DEC_SEQ = 8
PAST_LEN = 2048
PAGE_SIZE = 128

N_A_LAYERS = DEPTH // 2
N_B_LAYERS = DEPTH - N_A_LAYERS
HGRN_EXPAND = 128
H_A = D_MODEL // HGRN_EXPAND
DK_A = HGRN_EXPAND
DV_A = D_MODEL // H_A
CHUNK_A = 64
H_B = 16
HD_B = D_MODEL // H_B
Q_BLOCK = 128
D_FF = 4 * D_MODEL
EPS = 1e-6
SB_BIAS_INIT = -6.0

kernel_name = 'yoco_hgrn2_stickbreaking_decoder_step'


def rmsnorm(x, g):
    xf = x.astype(jnp.float32)
    y = xf * lax.rsqrt(jnp.mean(xf * xf, axis=-1, keepdims=True) + EPS) * g.astype(jnp.float32)
    return y.astype(x.dtype)


def sq_relu_mlp(xn, w_up, w_down):
    h = jax.nn.relu(xn @ w_up)
    return (h * h) @ w_down


def hgrn2_mixer(xn, w_in, lb, g_onorm, w_out, S0):
    f32 = jnp.float32
    B, L, _ = xn.shape
    proj = xn @ w_in
    q, fz, iv, og = jnp.split(proj, 4, axis=-1)
    q = jax.nn.silu(q.astype(f32)).reshape(B, L, H_A, DK_A)
    fz = fz.astype(f32).reshape(B, L, H_A, DK_A)
    lb = lb.astype(f32).reshape(H_A, DK_A)
    log_f = jnp.logaddexp(jnp.log(lb), jnp.log1p(-lb) + jax.nn.log_sigmoid(fz))
    k = (1.0 - lb) * jax.nn.sigmoid(-fz)
    v = iv.astype(f32).reshape(B, L, H_A, DV_A)
    C = math.gcd(L, CHUNK_A)
    N = L // C

    def to_chunks(a):
        return jnp.moveaxis(a.reshape(B, N, C, *a.shape[2:]), 1, 0)

    causal = jnp.tril(jnp.ones((C, C), dtype=bool))[None, :, :, None, None]

    def step(S, inp):
        qc, kc, vc, lfc = inp
        G = jnp.cumsum(lfc, axis=1)
        o_inter = jnp.einsum('bthk,bhkv->bthv', qc * jnp.exp(G), S)
        diff = G[:, :, None] - G[:, None, :]
        decay = jnp.exp(jnp.where(causal, diff, -jnp.inf))
        scores = jnp.einsum('bthk,bshk,btshk->bhts', qc, kc, decay)
        o_intra = jnp.einsum('bhts,bshv->bthv', scores, vc)
        G_last = G[:, -1]
        S_new = jnp.exp(G_last)[..., None] * S + jnp.einsum(
            'bshk,bshv->bhkv', kc * jnp.exp(G_last[:, None] - G), vc)
        return S_new, o_inter + o_intra

    S_fin, o = lax.scan(step, S0.astype(f32),
                        (to_chunks(q), to_chunks(k), to_chunks(v), to_chunks(log_f)))
    o = jnp.moveaxis(o, 0, 1).reshape(B, L, H_A, DV_A)
    o = o * lax.rsqrt(jnp.mean(o * o, axis=-1, keepdims=True) + EPS)
    o = o.reshape(B, L, D_MODEL) * g_onorm.astype(f32) * jax.nn.silu(og.astype(f32))
    return o.astype(xn.dtype) @ w_out, S_fin


def stick_breaking(q, k, v, bias, q_offset):
    f32 = jnp.float32
    Lq, Lk = q.shape[1], k.shape[1]
    z = jnp.einsum('bqhd,bkhd->bhqk', q.astype(f32), k.astype(f32)) * (HD_B ** -0.5)
    z = z + bias.astype(f32)[None, :, None, None]
    mask = jnp.arange(Lk)[None, :] < (q_offset + jnp.arange(Lq))[:, None]
    u = jnp.where(mask, jax.nn.log_sigmoid(-z), 0.0)
    rev_excl = lax.cumsum(u, axis=3, reverse=True) - u
    A = jnp.where(mask, jnp.exp(jax.nn.log_sigmoid(z) + rev_excl), 0.0)
    return jnp.einsum('bhqk,bkhd->bqhd', A, v.astype(f32))


def stick_breaking_blocked(q, k_all, v_all, bias, past_len):
    L = q.shape[1]
    qb = math.gcd(L, Q_BLOCK)
    outs = []
    for i in range(L // qb):
        e = past_len + (i + 1) * qb
        outs.append(stick_breaking(q[:, i * qb:(i + 1) * qb], k_all[:, :e], v_all[:, :e],
                                   bias, past_len + i * qb))
    return jnp.concatenate(outs, axis=1)


def run_trunk(x, S0s, past_k, past_v, lb_all, w_in_a, g_onorm_a, w_out_a, w_kv, g_kv,
              w_q_b, w_out_b, sb_bias, w_up, w_down, g_mix, g_mlp, g_final):
    B, L, _ = x.shape
    past_len = past_k.shape[1]
    h = x
    new_S = []
    k_new = None
    v_new = None
    k_all = None
    v_all = None
    for l in range(DEPTH):
        if l < N_A_LAYERS:
            y, S = hgrn2_mixer(rmsnorm(h, g_mix[l]), w_in_a[l], lb_all[l], g_onorm_a[l],
                               w_out_a[l], S0s[l])
            new_S.append(S)
            h = h + y
        else:
            if l == N_A_LAYERS:
                kv = rmsnorm(h, g_kv) @ w_kv
                k_new, v_new = jnp.split(kv, 2, axis=-1)
                k_new = k_new.reshape(B, L, H_B, HD_B)
                v_new = v_new.reshape(B, L, H_B, HD_B)
                k_all = jnp.concatenate([past_k.astype(k_new.dtype), k_new], axis=1)
                v_all = jnp.concatenate([past_v.astype(v_new.dtype), v_new], axis=1)
            j = l - N_A_LAYERS
            q = (rmsnorm(h, g_mix[l]) @ w_q_b[j]).reshape(B, L, H_B, HD_B)
            o = stick_breaking_blocked(q, k_all, v_all, sb_bias[j], past_len)
            h = h + o.reshape(B, L, D_MODEL).astype(h.dtype) @ w_out_b[j]
        h = h + sq_relu_mlp(rmsnorm(h, g_mlp[l]), w_up[l], w_down[l])
    return rmsnorm(h, g_final), jnp.stack(new_S), k_new, v_new


def setup_inputs(seed: int = 0) -> dict:
    key = jax.random.key(seed)
    ks = jax.random.split(key, 24)
    f32 = jnp.float32
    n_pages = PAST_LEN // PAGE_SIZE
    n_used = DEC_BATCH * n_pages
    n_pool = n_used + max(n_used // 4, 1)
    nrm = jax.random.normal
    page_table = jax.random.permutation(ks[0], n_pool)[:n_used].reshape(DEC_BATCH, n_pages).astype(jnp.int32)
    return {
        'x_prompt': nrm(ks[1], (BATCH, SEQ, D_MODEL), f32),
        'x_sample': nrm(ks[2], (DEC_BATCH, DEC_SEQ, D_MODEL), f32),
        'state_hgrn': 0.5 * nrm(ks[3], (N_A_LAYERS, DEC_BATCH, H_A, DK_A, DV_A), f32),
        'cache_k': nrm(ks[4], (n_pool, PAGE_SIZE, H_B, HD_B), f32),
        'cache_v': nrm(ks[5], (n_pool, PAGE_SIZE, H_B, HD_B), f32),
        'page_table': page_table,
        'lb_logits': 0.5 * nrm(ks[6], (N_A_LAYERS + 1, D_MODEL), f32),
        'w_in_a': nrm(ks[7], (N_A_LAYERS, D_MODEL, 4 * D_MODEL), f32) * D_MODEL ** -0.5,
        'g_onorm_a': 1.0 + 0.02 * nrm(ks[8], (N_A_LAYERS, D_MODEL), f32),
        'w_out_a': nrm(ks[9], (N_A_LAYERS, D_MODEL, D_MODEL), f32) * D_MODEL ** -0.5,
        'w_kv': nrm(ks[10], (D_MODEL, 2 * D_MODEL), f32) * D_MODEL ** -0.5,
        'g_kv': 1.0 + 0.02 * nrm(ks[11], (D_MODEL,), f32),
        'w_q_b': nrm(ks[12], (N_B_LAYERS, D_MODEL, D_MODEL), f32) * D_MODEL ** -0.5,
        'w_out_b': nrm(ks[13], (N_B_LAYERS, D_MODEL, D_MODEL), f32) * D_MODEL ** -0.5,
        'sb_bias': SB_BIAS_INIT + 0.1 * nrm(ks[19], (N_B_LAYERS, H_B), f32),
        'w_up': nrm(ks[14], (DEPTH, D_MODEL, D_FF), f32) * D_MODEL ** -0.5,
        'w_down': nrm(ks[15], (DEPTH, D_FF, D_MODEL), f32) * D_FF ** -0.5,
        'g_mix': 1.0 + 0.02 * nrm(ks[16], (DEPTH, D_MODEL), f32),
        'g_mlp': 1.0 + 0.02 * nrm(ks[17], (DEPTH, D_MODEL), f32),
        'g_final': 1.0 + 0.02 * nrm(ks[18], (D_MODEL,), f32),
    }


def reference(x_prompt, x_sample, state_hgrn, cache_k, cache_v, page_table, lb_logits, w_in_a,
              g_onorm_a, w_out_a, w_kv, g_kv, w_q_b, w_out_b, sb_bias, w_up, w_down, g_mix,
              g_mlp, g_final):
    lb_all = jnp.cumsum(jax.nn.softmax(lb_logits.astype(jnp.float32), axis=0), axis=0)
    weights = (lb_all, w_in_a, g_onorm_a, w_out_a, w_kv, g_kv, w_q_b, w_out_b, sb_bias,
               w_up, w_down, g_mix, g_mlp, g_final)
    Bp = x_prompt.shape[0]
    S0_prompt = jnp.zeros((N_A_LAYERS, Bp, H_A, DK_A, DV_A), jnp.float32)
    empty = jnp.zeros((Bp, 0, H_B, HD_B), x_prompt.dtype)
    y_prompt, state_hgrn_prompt, k_prompt, v_prompt = run_trunk(
        x_prompt, S0_prompt, empty, empty, *weights)
    Bs = page_table.shape[0]
    past_k = cache_k[page_table].reshape(Bs, -1, H_B, HD_B)
    past_v = cache_v[page_table].reshape(Bs, -1, H_B, HD_B)
    y_sample, state_hgrn_sample, k_sample, v_sample = run_trunk(
        x_sample, state_hgrn, past_k, past_v, *weights)
    return (y_prompt, y_sample, state_hgrn_prompt, state_hgrn_sample,
            k_prompt, v_prompt, k_sample, v_sample)
```

```python
import functools
import math

import jax
import jax.numpy as jnp
from jax import lax
from jax.experimental import pallas as pl
from jax.experimental.pallas import tpu as pltpu

F32 = jnp.float32
BF16 = jnp.bfloat16

D_MODEL = 1024
D_FF = 4 * D_MODEL
H_A = 8
DK_A = 128
H_B = 16
HD_B = 64
EPS = 1e-6
PAGE = 128
LANES = 128
SUBLANES = 8
VMEM_LIMIT = 56 * 1024 * 1024


def _dot(a, b):
    return jnp.dot(a, b, preferred_element_type=F32)


def _dot_nt(a, b):
    return lax.dot_general(a, b, (((1,), (1,)), ((), ())), preferred_element_type=F32)


def _dot_tn(a, b):
    return lax.dot_general(a, b, (((0,), (0,)), ((), ())), preferred_element_type=F32)


def _inv_rms(x):
    return lax.rsqrt(jnp.mean(x * x, axis=-1, keepdims=True) + EPS)


def _resident(shape):
    nd = len(shape)
    return pl.BlockSpec(shape, lambda *_: (0,) * nd, pipeline_mode=pl.Buffered(1))


def _params(*sem):
    return pltpu.CompilerParams(dimension_semantics=sem, vmem_limit_bytes=VMEM_LIMIT)


def _pre_kernel(x_ref, g_ref, w_ref, o_ref):
    x = x_ref[...]
    xn = (x * _inv_rms(x) * g_ref[...]).astype(BF16)
    o_ref[...] = _dot(xn, w_ref[...])


def _pre(x, g, w, tm):
    t, d = x.shape
    n = w.shape[1]
    return pl.pallas_call(
        _pre_kernel,
        grid=(t // tm,),
        in_specs=[pl.BlockSpec((tm, d), lambda i: (i, 0)),
                  _resident((1, d)),
                  _resident((d, n))],
        out_specs=pl.BlockSpec((tm, n), lambda i: (i, 0)),
        out_shape=jax.ShapeDtypeStruct((t, n), F32),
        compiler_params=_params("parallel"),
        name="pre_proj",
    )(x, g, w)


def _level_ref_rows(g, m, c):
    if m >= SUBLANES:
        g3 = g.reshape(c // (2 * m), 2 * m, LANES)
        r = jnp.broadcast_to(g3[:, m - 1:m, :], g3.shape)
        return r.reshape(c, LANES)
    g3 = g.reshape(c // SUBLANES, SUBLANES, LANES)
    rib = lax.broadcasted_iota(jnp.int32, g3.shape, 1)
    out = None
    for start in range(0, SUBLANES, 2 * m):
        r = jnp.broadcast_to(g3[:, start + m - 1:start + m, :], g3.shape)
        out = r if out is None else jnp.where(rib >= start, r, out)
    return out.reshape(c, LANES)


def _split3(x):
    hi = x.astype(BF16)
    r1 = x - hi.astype(F32)
    mid = r1.astype(BF16)
    lo = (r1 - mid.astype(F32)).astype(BF16)
    return hi, mid, lo


def _hgrn_chunk(qr, fz, iv, og, lb, gon, s_prev, c):
    row = lax.broadcasted_iota(jnp.int32, (c, c), 0)
    col = lax.broadcasted_iota(jnp.int32, (c, c), 1)
    xor = row ^ col
    lower = row > col

    qv = qr * (1.0 / (1.0 + jnp.exp(-qr)))
    e = jnp.exp(-jnp.abs(fz))
    r = 1.0 / (1.0 + e)
    er = e * r
    pos = fz >= 0.0
    sig = jnp.where(pos, r, er)
    nsig = jnp.where(pos, er, r)
    lf = jnp.log(lb + (1.0 - lb) * sig)
    kk = (1.0 - lb) * nsig

    tri = jnp.where(row >= col, 1.0, 0.0).astype(BF16)
    hi, mid, lo = _split3(lf)
    g = _dot(tri, hi) + _dot(tri, mid) + _dot(tri, lo)

    vb = iv.astype(BF16)
    o = _dot((qv * jnp.exp(g)).astype(BF16), s_prev.astype(BF16))

    scores = jnp.where(row == col, _dot_nt(qv.astype(BF16), kk.astype(BF16)), 0.0)
    m = 1
    while m < c:
        em = jnp.exp(-jnp.abs(g - _level_ref_rows(g, m, c)))
        pm = _dot_nt((qv * em).astype(BF16), (kk * em).astype(BF16))
        scores = jnp.where(lower & (xor >= m) & (xor < 2 * m), pm, scores)
        m *= 2
    o = o + _dot(scores.astype(BF16), vb)

    g_last = g[c - 1:c, :]
    kdec = (kk * jnp.exp(g_last - g)).astype(BF16)
    dh, dm, dl = (p.astype(F32) for p in _split3(jnp.exp(g_last)))
    rid = lax.broadcasted_iota(jnp.int32, (SUBLANES, LANES), 0)
    d3 = jnp.where(rid == 0, dh, jnp.where(rid == 1, dm, jnp.where(rid == 2, dl, 0.0)))
    decay = _dot_tn(d3.astype(BF16), jnp.ones((SUBLANES, LANES), BF16))
    s_new = decay * s_prev + _dot_tn(kdec, vb)

    on = o * _inv_rms(o) * gon * (og * (1.0 / (1.0 + jnp.exp(-og))))
    return on, s_new


def _hgrn_kernel(*refs, c, tb, hg, has_s0):
    if has_s0:
        q_ref, fz_ref, iv_ref, og_ref, lb_ref, gon_ref, s0_ref, o_ref, so_ref, s_scr = refs
    else:
        q_ref, fz_ref, iv_ref, og_ref, lb_ref, gon_ref, o_ref, so_ref, s_scr = refs
    ci = pl.program_id(2)

    @pl.when(ci == 0)
    def _init():
        if has_s0:
            s_scr[...] = s0_ref[0]
        else:
            s_scr[...] = jnp.zeros_like(s_scr)

    for h in range(hg):
        cols = slice(h * LANES, (h + 1) * LANES)
        lb = lb_ref[:, cols]
        gon = gon_ref[:, cols]

        def chunk(rows, h=h, cols=cols, lb=lb, gon=gon):
            on, s_new = _hgrn_chunk(q_ref[0, rows, cols], fz_ref[0, rows, cols],
                                    iv_ref[0, rows, cols], og_ref[0, rows, cols],
                                    lb, gon, s_scr[h], c)
            o_ref[0, rows, cols] = on.astype(o_ref.dtype)
            s_scr[h] = s_new

        if tb == c:
            chunk(slice(0, c))
        else:
            def body(j, carry):
                chunk(pl.ds(pl.multiple_of(j * c, c), c))
                return carry
            lax.fori_loop(0, tb // c, body, 0)

    @pl.when(ci == pl.num_programs(2) - 1)
    def _fin():
        so_ref[0] = s_scr[...]


def _hgrn(proj, lb, gon, s0, *, c, tb, hg, out_dtype):
    b, l, _ = proj.shape
    nhg = H_A // hg
    w = hg * LANES

    def tok_spec(k):
        return pl.BlockSpec((1, tb, w), lambda bi, hi, ci, k=k: (bi, ci, k * nhg + hi))

    vec_spec = pl.BlockSpec((1, w), lambda bi, hi, ci: (0, hi))
    st_spec = pl.BlockSpec((1, hg, DK_A, DK_A), lambda bi, hi, ci: (bi, hi, 0, 0))
    in_specs = [tok_spec(0), tok_spec(1), tok_spec(2), tok_spec(3), vec_spec, vec_spec]
    args = [proj, proj, proj, proj, lb, gon]
    if s0 is not None:
        in_specs.append(st_spec)
        args.append(s0)
    return pl.pallas_call(
        functools.partial(_hgrn_kernel, c=c, tb=tb, hg=hg, has_s0=s0 is not None),
        grid=(b, nhg, l // tb),
        in_specs=in_specs,
        out_specs=[pl.BlockSpec((1, tb, w), lambda bi, hi, ci: (bi, ci, hi)), st_spec],
        out_shape=[jax.ShapeDtypeStruct((b, l, D_MODEL), out_dtype),
                   jax.ShapeDtypeStruct((b, H_A, DK_A, DK_A), F32)],
        scratch_shapes=[pltpu.VMEM((hg, DK_A, DK_A), F32)],
        compiler_params=_params("parallel", "parallel", "arbitrary"),
        name="hgrn",
    )(*args)


def _mix_mlp(x_ref, o_ref, wo_ref, gm_ref, wu_ref, wd_ref):
    h1 = x_ref[...] + _dot(o_ref[...].astype(BF16), wo_ref[...])
    xn = (h1 * _inv_rms(h1) * gm_ref[...]).astype(BF16)
    h2 = h1
    for f in range(D_FF // D_MODEL):
        cols = slice(f * D_MODEL, (f + 1) * D_MODEL)
        u = jnp.maximum(_dot(xn, wu_ref[:, cols]), 0.0)
        h2 = h2 + _dot((u * u).astype(BF16), wd_ref[cols, :])
    return h2


def _post0_kernel(x_ref, o_ref, wo_ref, gm_ref, wu_ref, wd_ref, gkv_ref, wkv_ref, gq_ref,
                  wq_ref, h_ref, k_ref, v_ref, q_ref):
    h2 = _mix_mlp(x_ref, o_ref, wo_ref, gm_ref, wu_ref, wd_ref)
    h_ref[...] = h2
    hn = h2 * _inv_rms(h2)
    kv = _dot((hn * gkv_ref[...]).astype(BF16), wkv_ref[...])
    k_ref[...] = kv[:, :D_MODEL]
    v_ref[...] = kv[:, D_MODEL:]
    q_ref[...] = _dot((hn * gq_ref[...]).astype(BF16), wq_ref[...]) * (HD_B ** -0.5)


def _post1_kernel(x_ref, o_ref, wo_ref, gm_ref, wu_ref, wd_ref, gf_ref, y_ref):
    h2 = _mix_mlp(x_ref, o_ref, wo_ref, gm_ref, wu_ref, wd_ref)
    y_ref[...] = h2 * _inv_rms(h2) * gf_ref[...]


def _tok_spec(tm, n):
    return pl.BlockSpec((tm, n), lambda i: (i, 0))


def _post0(x, o, wo, gm, wu, wd, gkv, wkv, gq, wq, tm):
    t, d = x.shape
    weights = [wo, gm, wu, wd, gkv, wkv, gq, wq]
    return pl.pallas_call(
        _post0_kernel,
        grid=(t // tm,),
        in_specs=[_tok_spec(tm, d), _tok_spec(tm, d)] + [_resident(a.shape) for a in weights],
        out_specs=[_tok_spec(tm, d)] * 4,
        out_shape=[jax.ShapeDtypeStruct((t, d), F32)] * 4,
        compiler_params=_params("parallel"),
        name="post0",
    )(x, o, *weights)


def _post1(x, o, wo, gm, wu, wd, gf, tm):
    t, d = x.shape
    weights = [wo, gm, wu, wd, gf]
    return pl.pallas_call(
        _post1_kernel,
        grid=(t // tm,),
        in_specs=[_tok_spec(tm, d), _tok_spec(tm, d)] + [_resident(a.shape) for a in weights],
        out_specs=_tok_spec(tm, d),
        out_shape=jax.ShapeDtypeStruct((t, d), F32),
        compiler_params=_params("parallel"),
        name="post1",
    )(x, o, *weights)


def _sb_tile(z, carry, tt, mask):
    sp = jnp.log(1.0 + jnp.exp(-jnp.abs(z)))
    lsz = jnp.minimum(z, 0.0) - sp
    u = -(jnp.maximum(z, 0.0) + sp)
    if mask is not None:
        u = jnp.where(mask, u, 0.0)
    hi = u.astype(BF16)
    lo = (u - hi.astype(F32)).astype(BF16)
    r = _dot(jnp.concatenate([hi, lo], axis=1), tt)
    a = jnp.exp(lsz + r[:, :LANES] + carry)
    if mask is not None:
        a = jnp.where(mask, a, 0.0)
    return a, carry + r[:, LANES:]


def _attn_kernel(bias_ref, q_ref, k_ref, v_ref, tt_ref, o_ref, kb, vb, acc, carry, *, tq):
    hp = pl.program_id(1)
    qi = pl.program_id(2)

    @pl.when(qi == 0)
    def _cast():
        kb[...] = k_ref[0].astype(BF16)
        vb[...] = v_ref[0].astype(BF16)

    q = q_ref[0]
    lane = lax.broadcasted_iota(jnp.int32, (tq, LANES), 1)
    first = lane < HD_B
    q2 = jnp.concatenate([jnp.where(first, q, 0.0), jnp.where(first, 0.0, q)], axis=0).astype(BF16)
    rows2 = lax.broadcasted_iota(jnp.int32, (2 * tq, LANES), 0)
    bias2 = jnp.where(rows2 < tq, bias_ref[2 * hp], bias_ref[2 * hp + 1])
    tt = tt_ref[...]

    def block(j, mask):
        rows = pl.ds(pl.multiple_of(j * tq, tq), tq)
        z = _dot_nt(q2, kb[rows, :]) + bias2
        a, c_new = _sb_tile(z, carry[...], tt, mask)
        carry[...] = c_new
        return _dot(a.astype(BF16), vb[rows, :])

    carry[...] = jnp.zeros_like(carry)
    t_idx = jnp.where(rows2 < tq, rows2, rows2 - tq)
    s_idx = lax.broadcasted_iota(jnp.int32, (2 * tq, LANES), 1)
    acc[...] = block(qi, s_idx < t_idx)

    def body(jj, c):
        acc[...] += block(qi - 1 - jj, None)
        return c
    lax.fori_loop(0, qi, body, 0)

    o_ref[0] = jnp.where(first, acc[:tq, :], acc[tq:, :]).astype(o_ref.dtype)


def _tri_table():
    j = jnp.arange(LANES)[:, None]
    s = jnp.arange(LANES)[None, :]
    t = jnp.concatenate([(j > s).astype(BF16), jnp.ones((LANES, LANES), BF16)], axis=1)
    return jnp.concatenate([t, t], axis=0)


def _attn_prompt(q, k, v, bias, tq):
    b, l, d = q.shape
    nhp = d // LANES
    kv_spec = pl.BlockSpec((1, l, LANES), lambda bi, hp, qi: (bi, 0, hp))
    return pl.pallas_call(
        functools.partial(_attn_kernel, tq=tq),
        grid=(b, nhp, l // tq),
        in_specs=[pl.BlockSpec(memory_space=pltpu.SMEM),
                  pl.BlockSpec((1, tq, LANES), lambda bi, hp, qi: (bi, qi, hp)),
                  kv_spec, kv_spec,
                  _resident((2 * LANES, 2 * LANES))],
        out_specs=pl.BlockSpec((1, tq, LANES), lambda bi, hp, qi: (bi, qi, hp)),
        out_shape=jax.ShapeDtypeStruct((b, l, d), BF16),
        scratch_shapes=[pltpu.VMEM((l, LANES), BF16), pltpu.VMEM((l, LANES), BF16),
                        pltpu.VMEM((2 * tq, LANES), F32), pltpu.VMEM((2 * tq, LANES), F32)],
        compiler_params=_params("parallel", "parallel", "arbitrary"),
        name="attn_prompt",
    )(bias, q, k, v, _tri_table())


def _sattn_kernel(pt_ref, q_ref, kn_ref, vn_ref, kp_ref, vp_ref, bias_ref, tt_ref, o_ref,
                  qbd, acc_new, acc_t, carry, *, lq):
    si = pl.program_id(1)
    rows = H_B * lq
    row = lax.broadcasted_iota(jnp.int32, (rows, D_MODEL), 0)
    col = lax.broadcasted_iota(jnp.int32, (rows, D_MODEL), 1)
    own = (col // HD_B) == (row // lq)
    tt = tt_ref[...]

    @pl.when(si == 0)
    def _new_keys():
        q16 = jnp.concatenate([q_ref[0]] * H_B, axis=0)
        qbd[...] = jnp.where(own, q16, 0.0).astype(BF16)
        pad = jnp.zeros((PAGE - lq, D_MODEL), F32)
        kblk = jnp.concatenate([kn_ref[0], pad], axis=0).astype(BF16)
        vblk = jnp.concatenate([vn_ref[0], pad], axis=0).astype(BF16)
        t_idx = lax.broadcasted_iota(jnp.int32, (rows, LANES), 0) % lq
        s_idx = lax.broadcasted_iota(jnp.int32, (rows, LANES), 1)
        z = _dot_nt(qbd[...], kblk) + bias_ref[...]
        a, c_new = _sb_tile(z, jnp.zeros((rows, LANES), F32), tt, s_idx < t_idx)
        carry[...] = c_new
        acc_new[...] = _dot(a.astype(BF16), vblk)
        acc_t[...] = jnp.zeros_like(acc_t)

    @pl.when(si > 0)
    def _page():
        z = _dot(qbd[...], kp_ref[0].astype(BF16)) + bias_ref[...]
        a, c_new = _sb_tile(z, carry[...], tt, None)
        carry[...] = c_new
        acc_t[...] += _dot_nt(vp_ref[0].astype(BF16), a.astype(BF16))

    @pl.when(si == pl.num_programs(1) - 1)
    def _fin():
        total = acc_new[...] + acc_t[...].T
        sel = jnp.where(own, total, 0.0).reshape(H_B, lq, D_MODEL)
        o_ref[0] = jnp.sum(sel, axis=0)


def _attn_sample(q, kn, vn, cache_k, cache_v, page_table, bias):
    b, lq, d = q.shape
    n_pages = page_table.shape[1]
    rows = H_B * lq
    bias_rows = jnp.broadcast_to(jnp.repeat(bias, lq)[:, None], (rows, LANES))

    def page_map(bi, si, pt):
        return (pt[bi, n_pages - 1 - jnp.maximum(si - 1, 0)], 0, 0)

    new_spec = pl.BlockSpec((1, lq, d), lambda bi, si, pt: (bi, 0, 0))
    page_spec = pl.BlockSpec((1, d, PAGE), page_map)
    grid_spec = pltpu.PrefetchScalarGridSpec(
        num_scalar_prefetch=1,
        grid=(b, n_pages + 1),
        in_specs=[new_spec, new_spec, new_spec, page_spec, page_spec,
                  pl.BlockSpec((rows, LANES), lambda bi, si, pt: (0, 0)),
                  pl.BlockSpec((2 * LANES, 2 * LANES), lambda bi, si, pt: (0, 0))],
        out_specs=new_spec,
        scratch_shapes=[pltpu.VMEM((rows, d), BF16), pltpu.VMEM((rows, d), F32),
                        pltpu.VMEM((d, rows), F32), pltpu.VMEM((rows, LANES), F32)],
    )
    return pl.pallas_call(
        functools.partial(_sattn_kernel, lq=lq),
        grid_spec=grid_spec,
        out_shape=jax.ShapeDtypeStruct((b, lq, d), F32),
        compiler_params=_params("parallel", "arbitrary"),
        name="attn_sample",
    )(page_table, q, kn, vn, cache_k, cache_v, bias_rows, _tri_table())


def _trunk(x, s0, past, w, *, tm, hgrn_cfg):
    b, l, d = x.shape
    xt = x.reshape(b * l, d)
    proj = _pre(xt, w["g_mix0"], w["w_in"], tm)
    o, s_new = _hgrn(proj.reshape(b, l, 4 * d), w["lb"], w["g_onorm"], s0, **hgrn_cfg)
    h, k, v, q = _post0(xt, o.reshape(b * l, d), w["w_out_a"], w["g_mlp0"], w["w_up0"],
                        w["w_down0"], w["g_kv"], w["w_kv"], w["g_mix1"], w["w_q"], tm)
    k3, v3, q3 = (a.reshape(b, l, d) for a in (k, v, q))
    if past is None:
        att = _attn_prompt(q3, k3, v3, w["sb_bias"], tq=128)
    else:
        cache_k, cache_v, page_table = past
        att = _attn_sample(q3, k3, v3, cache_k, cache_v, page_table, w["sb_bias"])
    y = _post1(h, att.reshape(b * l, d), w["w_out_b"], w["g_mlp1"], w["w_up1"], w["w_down1"],
               w["g_final"], tm)
    return (y.reshape(b, l, d), s_new[None], k.reshape(b, l, H_B, HD_B),
            v.reshape(b, l, H_B, HD_B))


def kernel(x_prompt, x_sample, state_hgrn, cache_k, cache_v, page_table, lb_logits, w_in_a,
           g_onorm_a, w_out_a, w_kv, g_kv, w_q_b, w_out_b, sb_bias, w_up, w_down, g_mix,
           g_mlp, g_final):
    lb_all = jnp.cumsum(jax.nn.softmax(lb_logits.astype(F32), axis=0), axis=0)
    row = lambda a: a.reshape(1, -1).astype(F32)
    w = {
        "lb": row(lb_all[0]), "g_mix0": row(g_mix[0]), "g_mix1": row(g_mix[1]),
        "g_mlp0": row(g_mlp[0]), "g_mlp1": row(g_mlp[1]), "g_kv": row(g_kv),
        "g_final": row(g_final), "g_onorm": row(g_onorm_a[0]),
        "sb_bias": sb_bias[0].astype(F32),
        "w_in": w_in_a[0].astype(BF16), "w_out_a": w_out_a[0].astype(BF16),
        "w_kv": w_kv.astype(BF16), "w_q": w_q_b[0].astype(BF16),
        "w_out_b": w_out_b[0].astype(BF16),
        "w_up0": w_up[0].astype(BF16), "w_up1": w_up[1].astype(BF16),
        "w_down0": w_down[0].astype(BF16), "w_down1": w_down[1].astype(BF16),
    }
    n_pool = cache_k.shape[0]
    page_t = lambda a: jnp.transpose(a, (0, 2, 3, 1)).reshape(n_pool, D_MODEL, PAGE)
    past = (page_t(cache_k), page_t(cache_v), page_table)
    ls = x_sample.shape[1]
    y_p, s_p, k_p, v_p = _trunk(
        x_prompt, None, None, w, tm=256,
        hgrn_cfg=dict(c=64, tb=256, hg=1, out_dtype=BF16))
    y_s, s_s, k_s, v_s = _trunk(
        x_sample, state_hgrn[0], past, w, tm=256,
        hgrn_cfg=dict(c=ls, tb=ls, hg=H_A, out_dtype=F32))
    return (y_p, y_s, s_p, s_s, k_p, v_p, k_s, v_s)
```

```python
import functools
import math

import jax
import jax.numpy as jnp
from jax import lax
from jax.experimental import pallas as pl
from jax.experimental.pallas import tpu as pltpu

F32 = jnp.float32
BF16 = jnp.bfloat16

D_MODEL = 1024
D_FF = 4 * D_MODEL
H_A = 8
DK_A = 128
H_B = 16
HD_B = 64
EPS = 1e-6
PAGE = 128
LANES = 128
SUBLANES = 8
VMEM_LIMIT = 56 * 1024 * 1024
LOG2E = math.log2(math.e)

DENSE_TM = 256
HGRN_PROMPT = dict(c=64, tb=256, hg=4)
ATTN_TQ = 256
PAGES_PER_STEP = 4


def _dot(a, b):
    return jnp.dot(a, b, preferred_element_type=F32)


def _dot_nt(a, b):
    return lax.dot_general(a, b, (((1,), (1,)), ((), ())), preferred_element_type=F32)


def _dot_tn(a, b):
    return lax.dot_general(a, b, (((0,), (0,)), ((), ())), preferred_element_type=F32)


def _inv_rms(x):
    return lax.rsqrt(jnp.mean(x * x, axis=-1, keepdims=True) + EPS)


def _silu(x):
    return x * (1.0 / (1.0 + jnp.exp(-x)))


def _resident(shape):
    nd = len(shape)
    return pl.BlockSpec(shape, lambda *_: (0,) * nd, pipeline_mode=pl.Buffered(1))


def _params(*sem):
    return pltpu.CompilerParams(dimension_semantics=sem, vmem_limit_bytes=VMEM_LIMIT)


def _pre_kernel(x_ref, g_ref, w_ref, o_ref):
    x = x_ref[...]
    xn = (x * _inv_rms(x) * g_ref[...]).astype(BF16)
    o_ref[...] = _dot(xn, w_ref[...])


def _pre(x, g, w, tm):
    t, d = x.shape
    n = w.shape[1]
    return pl.pallas_call(
        _pre_kernel,
        grid=(t // tm,),
        in_specs=[pl.BlockSpec((tm, d), lambda i: (i, 0)),
                  _resident((1, d)),
                  _resident((d, n))],
        out_specs=pl.BlockSpec((tm, n), lambda i: (i, 0)),
        out_shape=jax.ShapeDtypeStruct((t, n), F32),
        compiler_params=_params("parallel"),
        name="pre_proj",
    )(x, g, w)


def _level_ref_rows(g, m):
    c, w = g.shape
    if m >= SUBLANES:
        g3 = g.reshape(c // (2 * m), 2 * m, w)
        r = jnp.broadcast_to(g3[:, m - 1:m, :], g3.shape)
        return r.reshape(c, w)
    g3 = g.reshape(c // SUBLANES, SUBLANES, w)
    rib = lax.broadcasted_iota(jnp.int32, g3.shape, 1)
    out = None
    for start in range(0, SUBLANES, 2 * m):
        r = jnp.broadcast_to(g3[:, start + m - 1:start + m, :], g3.shape)
        out = r if out is None else jnp.where(rib >= start, r, out)
    return out.reshape(c, w)


def _split3(x):
    hi = x.astype(BF16)
    r1 = x - hi.astype(F32)
    mid = r1.astype(BF16)
    lo = (r1 - mid.astype(F32)).astype(BF16)
    return hi, mid, lo


def _hgrn_chunk(qr, fz, iv, og, lb, gon, s_prev):
    c, w = qr.shape
    hg = w // LANES
    head = lambda a, h: a[:, h * LANES:(h + 1) * LANES]
    row = lax.broadcasted_iota(jnp.int32, (c, c), 0)
    col = lax.broadcasted_iota(jnp.int32, (c, c), 1)
    xor = row ^ col
    lower = row > col

    qv = _silu(qr)
    e = jnp.exp(-jnp.abs(fz))
    r = 1.0 / (1.0 + e)
    er = e * r
    pos = fz >= 0.0
    sig = jnp.where(pos, r, er)
    nsig = jnp.where(pos, er, r)
    lf = jnp.log(lb + (1.0 - lb) * sig)
    kk = (1.0 - lb) * nsig

    tri = jnp.where(row >= col, 1.0, 0.0).astype(BF16)
    gcat = _dot(tri, jnp.concatenate(_split3(lf), axis=1))
    g = gcat[:, :w] + gcat[:, w:2 * w] + gcat[:, 2 * w:]

    vb = iv.astype(BF16)
    qg = (qv * jnp.exp(g)).astype(BF16)
    qb = qv.astype(BF16)
    kb = kk.astype(BF16)

    scores = [jnp.where(row == col, _dot_nt(head(qb, h), head(kb, h)), 0.0) for h in range(hg)]
    m = 1
    while m < c:
        em = jnp.exp(-jnp.abs(g - _level_ref_rows(g, m)))
        qe = (qv * em).astype(BF16)
        ke = (kk * em).astype(BF16)
        sel = lower & (xor >= m) & (xor < 2 * m)
        scores = [jnp.where(sel, _dot_nt(head(qe, h), head(ke, h)), scores[h])
                  for h in range(hg)]
        m *= 2
    o = [_dot(head(qg, h), s_prev[h].astype(BF16)) + _dot(scores[h].astype(BF16), head(vb, h))
         for h in range(hg)]

    g_last = g[c - 1:c, :]
    kdec = (kk * jnp.exp(g_last - g)).astype(BF16)
    dh, dm, dl = (p.astype(F32) for p in _split3(jnp.exp(g_last)))
    rid = lax.broadcasted_iota(jnp.int32, (SUBLANES, w), 0)
    d3 = jnp.where(rid == 0, dh, jnp.where(rid == 1, dm, jnp.where(rid == 2, dl, 0.0)))
    d3 = d3.astype(BF16)
    ones = jnp.ones((SUBLANES, LANES), BF16)
    s_new = [_dot_tn(head(d3, h), ones) * s_prev[h] + _dot_tn(head(kdec, h), head(vb, h))
             for h in range(hg)]

    on = jnp.concatenate([o[h] * _inv_rms(o[h]) for h in range(hg)], axis=1)
    return on * gon * _silu(og), s_new


def _hgrn_kernel(*refs, c, tb, hg, has_s0):
    if has_s0:
        q_ref, fz_ref, iv_ref, og_ref, lb_ref, gon_ref, s0_ref, o_ref, so_ref, s_scr = refs
    else:
        q_ref, fz_ref, iv_ref, og_ref, lb_ref, gon_ref, o_ref, so_ref, s_scr = refs
    ci = pl.program_id(2)

    @pl.when(ci == 0)
    def _init():
        if has_s0:
            s_scr[...] = s0_ref[0]
        else:
            s_scr[...] = jnp.zeros_like(s_scr)

    def chunk(rows):
        on, s_new = _hgrn_chunk(q_ref[0, rows, :], fz_ref[0, rows, :], iv_ref[0, rows, :],
                                og_ref[0, rows, :], lb_ref[...], gon_ref[...],
                                [s_scr[h] for h in range(hg)])
        o_ref[0, rows, :] = on.astype(o_ref.dtype)
        for h in range(hg):
            s_scr[h] = s_new[h]

    if tb == c:
        chunk(slice(0, c))
    else:
        def body(j, carry):
            chunk(pl.ds(pl.multiple_of(j * c, c), c))
            return carry
        lax.fori_loop(0, tb // c, body, 0)

    @pl.when(ci == pl.num_programs(2) - 1)
    def _fin():
        so_ref[0] = s_scr[...]


def _hgrn(proj, lb, gon, s0, *, c, tb, hg, out_dtype):
    b, l, _ = proj.shape
    nhg = H_A // hg
    w = hg * LANES

    def tok_spec(k):
        return pl.BlockSpec((1, tb, w), lambda bi, hi, ci, k=k: (bi, ci, k * nhg + hi))

    vec_spec = pl.BlockSpec((1, w), lambda bi, hi, ci: (0, hi))
    st_spec = pl.BlockSpec((1, hg, DK_A, DK_A), lambda bi, hi, ci: (bi, hi, 0, 0))
    in_specs = [tok_spec(0), tok_spec(1), tok_spec(2), tok_spec(3), vec_spec, vec_spec]
    args = [proj, proj, proj, proj, lb, gon]
    if s0 is not None:
        in_specs.append(st_spec)
        args.append(s0)
    return pl.pallas_call(
        functools.partial(_hgrn_kernel, c=c, tb=tb, hg=hg, has_s0=s0 is not None),
        grid=(b, nhg, l // tb),
        in_specs=in_specs,
        out_specs=[pl.BlockSpec((1, tb, w), lambda bi, hi, ci: (bi, ci, hi)), st_spec],
        out_shape=[jax.ShapeDtypeStruct((b, l, D_MODEL), out_dtype),
                   jax.ShapeDtypeStruct((b, H_A, DK_A, DK_A), F32)],
        scratch_shapes=[pltpu.VMEM((hg, DK_A, DK_A), F32)],
        compiler_params=_params("parallel", "parallel", "arbitrary"),
        name="hgrn",
    )(*args)


def _mix_mlp(x_ref, o_ref, wo_ref, gm_ref, wu_ref, wd_ref):
    h1 = x_ref[...] + _dot(o_ref[...].astype(BF16), wo_ref[...])
    xn = (h1 * _inv_rms(h1) * gm_ref[...]).astype(BF16)
    h2 = h1
    for f in range(D_FF // D_MODEL):
        cols = slice(f * D_MODEL, (f + 1) * D_MODEL)
        u = jnp.maximum(_dot(xn, wu_ref[:, cols]), 0.0)
        h2 = h2 + _dot((u * u).astype(BF16), wd_ref[cols, :])
    return h2


def _post0_kernel(x_ref, o_ref, wo_ref, gm_ref, wu_ref, wd_ref, gkv_ref, wkv_ref, gq_ref,
                  wq_ref, h_ref, k_ref, v_ref, q_ref):
    h2 = _mix_mlp(x_ref, o_ref, wo_ref, gm_ref, wu_ref, wd_ref)
    h_ref[...] = h2
    hn = h2 * _inv_rms(h2)
    kv = _dot((hn * gkv_ref[...]).astype(BF16), wkv_ref[...])
    k_ref[...] = kv[:, :D_MODEL]
    v_ref[...] = kv[:, D_MODEL:]
    q_ref[...] = _dot((hn * gq_ref[...]).astype(BF16), wq_ref[...]) * (HD_B ** -0.5 * LOG2E)


def _post1_kernel(x_ref, o_ref, wo_ref, gm_ref, wu_ref, wd_ref, gf_ref, y_ref):
    h2 = _mix_mlp(x_ref, o_ref, wo_ref, gm_ref, wu_ref, wd_ref)
    y_ref[...] = h2 * _inv_rms(h2) * gf_ref[...]


def _tok_spec(tm, n):
    return pl.BlockSpec((tm, n), lambda i: (i, 0))


def _post0(x, o, wo, gm, wu, wd, gkv, wkv, gq, wq, tm):
    t, d = x.shape
    weights = [wo, gm, wu, wd, gkv, wkv, gq, wq]
    return pl.pallas_call(
        _post0_kernel,
        grid=(t // tm,),
        in_specs=[_tok_spec(tm, d), _tok_spec(tm, d)] + [_resident(a.shape) for a in weights],
        out_specs=[_tok_spec(tm, d)] * 4,
        out_shape=[jax.ShapeDtypeStruct((t, d), F32)] * 4,
        compiler_params=_params("parallel"),
        name="post0",
    )(x, o, *weights)


def _post1(x, o, wo, gm, wu, wd, gf, tm):
    t, d = x.shape
    weights = [wo, gm, wu, wd, gf]
    return pl.pallas_call(
        _post1_kernel,
        grid=(t // tm,),
        in_specs=[_tok_spec(tm, d), _tok_spec(tm, d)] + [_resident(a.shape) for a in weights],
        out_specs=_tok_spec(tm, d),
        out_shape=jax.ShapeDtypeStruct((t, d), F32),
        compiler_params=_params("parallel"),
        name="post1",
    )(x, o, *weights)


def _sb_group(y_all, bias, carry, tt, masks):
    u = y_all.shape[1] // LANES
    parts = [None] * u
    for i in reversed(range(u)):
        y = y_all[:, i * LANES:(i + 1) * LANES] + bias
        neg_abs = lax.bitcast_convert_type(
            lax.bitcast_convert_type(y, jnp.uint32) | jnp.uint32(0x80000000), F32)
        v = jnp.maximum(y, 0.0) + jnp.log2(1.0 + jnp.exp2(neg_abs))
        if masks is not None:
            v = jnp.where(masks[i], v, 0.0)
        hi = v.astype(BF16)
        lo = (v - hi.astype(F32)).astype(BF16)
        r = _dot(jnp.concatenate([hi, lo], axis=1), tt)
        a = jnp.exp2(y - r[:, :LANES] - carry)
        if masks is not None:
            a = jnp.where(masks[i], a, 0.0)
        parts[i] = a.astype(BF16)
        carry = carry + r[:, LANES:]
    return jnp.concatenate(parts, axis=1), carry


def _attn_kernel(bias_ref, q_ref, k_ref, v_ref, tt_ref, o_ref, kb, vb, acc, carry, *, tq):
    hp = pl.program_id(1)
    qi = pl.program_id(2)
    nsub = tq // LANES
    assert nsub == 2

    @pl.when(qi == 0)
    def _cast():
        kb[...] = k_ref[0].astype(BF16)
        vb[...] = v_ref[0].astype(BF16)

    q = q_ref[0]
    lane = lax.broadcasted_iota(jnp.int32, (tq, LANES), 1)
    first = lane < HD_B
    q2 = jnp.concatenate([jnp.where(first, q, 0.0), jnp.where(first, 0.0, q)], axis=0).astype(BF16)
    rows2 = lax.broadcasted_iota(jnp.int32, (2 * tq, LANES), 0)
    bias2 = jnp.where(rows2 < tq, bias_ref[2 * hp], bias_ref[2 * hp + 1]) * LOG2E
    tt = tt_ref[...]

    def group(j0, u, masks):
        rows = pl.ds(pl.multiple_of(j0 * LANES, LANES), u * LANES)
        a, c_new = _sb_group(_dot_nt(q2, kb[rows, :]), bias2, carry[...], tt, masks)
        carry[...] = c_new
        return _dot(a, vb[rows, :])

    carry[...] = jnp.zeros_like(carry)
    t_idx = jnp.where(rows2 < tq, rows2, rows2 - tq)
    s_idx = lax.broadcasted_iota(jnp.int32, (2 * tq, LANES), 1)
    acc[...] = group(qi * nsub, nsub, [s_idx + i * LANES < t_idx for i in range(nsub)])

    @pl.when(qi % 2 == 1)
    def _pair():
        acc[...] += group(qi * nsub - 2, 2, None)

    top = (qi // 2) * 4

    def body(g, c):
        acc[...] += group(top - 4 * (g + 1), 4, None)
        return c
    lax.fori_loop(0, qi // 2, body, 0)

    o_ref[0] = jnp.where(first, acc[:tq, :], acc[tq:, :]).astype(o_ref.dtype)


def _tri_table():
    j = jnp.arange(LANES)[:, None]
    s = jnp.arange(LANES)[None, :]
    t = jnp.concatenate([(j >= s).astype(BF16), jnp.ones((LANES, LANES), BF16)], axis=1)
    return jnp.concatenate([t, t], axis=0)


def _attn_prompt(q, k, v, bias, tq):
    b, l, d = q.shape
    nhp = d // LANES
    kv_spec = pl.BlockSpec((1, l, LANES), lambda bi, hp, qi: (bi, 0, hp))
    return pl.pallas_call(
        functools.partial(_attn_kernel, tq=tq),
        grid=(b, nhp, l // tq),
        in_specs=[pl.BlockSpec(memory_space=pltpu.SMEM),
                  pl.BlockSpec((1, tq, LANES), lambda bi, hp, qi: (bi, qi, hp)),
                  kv_spec, kv_spec,
                  _resident((2 * LANES, 2 * LANES))],
        out_specs=pl.BlockSpec((1, tq, LANES), lambda bi, hp, qi: (bi, qi, hp)),
        out_shape=jax.ShapeDtypeStruct((b, l, d), BF16),
        scratch_shapes=[pltpu.VMEM((l, LANES), BF16), pltpu.VMEM((l, LANES), BF16),
                        pltpu.VMEM((2 * tq, LANES), F32), pltpu.VMEM((2 * tq, LANES), F32)],
        compiler_params=_params("parallel", "parallel", "arbitrary"),
        name="attn_prompt",
    )(bias, q, k, v, _tri_table())


def _sattn_kernel(pt_ref, q_ref, kn_ref, vn_ref, *rest, lq, npp):
    kp_refs, vp_refs = rest[:npp], rest[npp:2 * npp]
    bias_ref, tt_ref, o_ref, qbd, acc_new, acc_t, carry = rest[2 * npp:]
    si = pl.program_id(1)
    rows = H_B * lq
    row = lax.broadcasted_iota(jnp.int32, (rows, D_MODEL), 0)
    col = lax.broadcasted_iota(jnp.int32, (rows, D_MODEL), 1)
    own = (col // HD_B) == (row // lq)
    tt = tt_ref[...]

    @pl.when(si == 0)
    def _new_keys():
        q16 = jnp.concatenate([q_ref[0]] * H_B, axis=0)
        qbd[...] = jnp.where(own, q16, 0.0).astype(BF16)
        pad = jnp.zeros((PAGE - lq, D_MODEL), F32)
        kblk = jnp.concatenate([kn_ref[0], pad], axis=0).astype(BF16)
        vblk = jnp.concatenate([vn_ref[0], pad], axis=0).astype(BF16)
        t_idx = lax.broadcasted_iota(jnp.int32, (rows, LANES), 0) % lq
        s_idx = lax.broadcasted_iota(jnp.int32, (rows, LANES), 1)
        a, c_new = _sb_group(_dot_nt(qbd[...], kblk), bias_ref[...],
                             jnp.zeros((rows, LANES), F32), tt, [s_idx < t_idx])
        carry[...] = c_new
        acc_new[...] = _dot(a, vblk)
        acc_t[...] = jnp.zeros_like(acc_t)

    @pl.when(si > 0)
    def _pages():
        kcat = jnp.concatenate([kp_refs[i][0].astype(BF16) for i in reversed(range(npp))], axis=1)
        a, c_new = _sb_group(_dot(qbd[...], kcat), bias_ref[...], carry[...], tt, None)
        carry[...] = c_new
        vcat = jnp.concatenate([vp_refs[i][0].astype(BF16) for i in reversed(range(npp))], axis=1)
        acc_t[...] += _dot_nt(vcat, a)

    @pl.when(si == pl.num_programs(1) - 1)
    def _fin():
        total = acc_new[...] + acc_t[...].T
        sel = jnp.where(own, total, 0.0).reshape(H_B, lq, D_MODEL)
        o_ref[0] = jnp.sum(sel, axis=0)


def _attn_sample(q, kn, vn, cache_kt, cache_vt, page_table, bias, npp):
    b, lq, d = q.shape
    n_pages = page_table.shape[1]
    rows = H_B * lq
    bias_rows = jnp.broadcast_to(jnp.repeat(bias * LOG2E, lq)[:, None], (rows, LANES))

    def page_spec(i):
        def page_map(bi, si, pt):
            return (pt[bi, n_pages - 1 - jnp.maximum(si - 1, 0) * npp - i], 0, 0)
        return pl.BlockSpec((1, d, PAGE), page_map)

    new_spec = pl.BlockSpec((1, lq, d), lambda bi, si, pt: (bi, 0, 0))
    pages = [page_spec(i) for i in range(npp)]
    grid_spec = pltpu.PrefetchScalarGridSpec(
        num_scalar_prefetch=1,
        grid=(b, n_pages // npp + 1),
        in_specs=[new_spec, new_spec, new_spec] + pages + pages + [
            pl.BlockSpec((rows, LANES), lambda bi, si, pt: (0, 0)),
            pl.BlockSpec((2 * LANES, 2 * LANES), lambda bi, si, pt: (0, 0))],
        out_specs=new_spec,
        scratch_shapes=[pltpu.VMEM((rows, d), BF16), pltpu.VMEM((rows, d), F32),
                        pltpu.VMEM((d, rows), F32), pltpu.VMEM((rows, LANES), F32)],
    )
    return pl.pallas_call(
        functools.partial(_sattn_kernel, lq=lq, npp=npp),
        grid_spec=grid_spec,
        out_shape=jax.ShapeDtypeStruct((b, lq, d), F32),
        compiler_params=_params("parallel", "arbitrary"),
        name="attn_sample",
    )(page_table, q, kn, vn, *([cache_kt] * npp), *([cache_vt] * npp), bias_rows, _tri_table())


def _trunk(x, s0, past, w, *, hgrn_cfg):
    b, l, d = x.shape
    tm = DENSE_TM
    xt = x.reshape(b * l, d)
    proj = _pre(xt, w["g_mix0"], w["w_in"], tm)
    o, s_new = _hgrn(proj.reshape(b, l, 4 * d), w["lb"], w["g_onorm"], s0, **hgrn_cfg)
    h, k, v, q = _post0(xt, o.reshape(b * l, d), w["w_out_a"], w["g_mlp0"], w["w_up0"],
                        w["w_down0"], w["g_kv"], w["w_kv"], w["g_mix1"], w["w_q"], tm)
    k3, v3, q3 = (a.reshape(b, l, d) for a in (k, v, q))
    if past is None:
        att = _attn_prompt(q3, k3, v3, w["sb_bias"], ATTN_TQ)
    else:
        cache_kt, cache_vt, page_table = past
        att = _attn_sample(q3, k3, v3, cache_kt, cache_vt, page_table, w["sb_bias"],
                           PAGES_PER_STEP)
    y = _post1(h, att.reshape(b * l, d), w["w_out_b"], w["g_mlp1"], w["w_up1"], w["w_down1"],
               w["g_final"], tm)
    return (y.reshape(b, l, d), s_new[None], k.reshape(b, l, H_B, HD_B),
            v.reshape(b, l, H_B, HD_B))


def kernel(x_prompt, x_sample, state_hgrn, cache_k, cache_v, page_table, lb_logits, w_in_a,
           g_onorm_a, w_out_a, w_kv, g_kv, w_q_b, w_out_b, sb_bias, w_up, w_down, g_mix,
           g_mlp, g_final):
    lb_all = jnp.cumsum(jax.nn.softmax(lb_logits.astype(F32), axis=0), axis=0)
    row = lambda a: a.reshape(1, -1).astype(F32)
    w = {
        "lb": row(lb_all[0]), "g_mix0": row(g_mix[0]), "g_mix1": row(g_mix[1]),
        "g_mlp0": row(g_mlp[0]), "g_mlp1": row(g_mlp[1]), "g_kv": row(g_kv),
        "g_final": row(g_final), "g_onorm": row(g_onorm_a[0]),
        "sb_bias": sb_bias[0].astype(F32),
        "w_in": w_in_a[0].astype(BF16), "w_out_a": w_out_a[0].astype(BF16),
        "w_kv": w_kv.astype(BF16), "w_q": w_q_b[0].astype(BF16),
        "w_out_b": w_out_b[0].astype(BF16),
        "w_up0": w_up[0].astype(BF16), "w_up1": w_up[1].astype(BF16),
        "w_down0": w_down[0].astype(BF16), "w_down1": w_down[1].astype(BF16),
    }
    n_pool = cache_k.shape[0]
    page_t = lambda a: jnp.transpose(a, (0, 2, 3, 1)).reshape(n_pool, D_MODEL, PAGE)
    past = (page_t(cache_k), page_t(cache_v), page_table)
    ls = x_sample.shape[1]
    y_p, s_p, k_p, v_p = _trunk(x_prompt, None, None, w,
                                hgrn_cfg=dict(out_dtype=BF16, **HGRN_PROMPT))
    y_s, s_s, k_s, v_s = _trunk(x_sample, state_hgrn[0], past, w,
                                hgrn_cfg=dict(c=ls, tb=ls, hg=H_A, out_dtype=F32))
    return (y_p, y_s, s_p, s_s, k_p, v_p, k_s, v_s)
```

```python
import functools
import math

import jax
import jax.numpy as jnp
from jax import lax
from jax.experimental import pallas as pl
from jax.experimental.pallas import tpu as pltpu

F32 = jnp.float32
BF16 = jnp.bfloat16

D_MODEL = 1024
D_FF = 4 * D_MODEL
H_A = 8
DK_A = 128
H_B = 16
HD_B = 64
EPS = 1e-6
PAGE = 128
LANES = 128
SUBLANES = 8
VMEM_LIMIT = 56 * 1024 * 1024
LOG2E = math.log2(math.e)

PRE_TM = 512
POST0_TM = 256
POST1_TM = 512
HGRN_PROMPT = dict(c=64, tb=256, hg=4)
ATTN_TQ = 256
PAGES_PER_STEP = 8


def _dot(a, b):
    return jnp.dot(a, b, preferred_element_type=F32)


def _dot_nt(a, b):
    return lax.dot_general(a, b, (((1,), (1,)), ((), ())), preferred_element_type=F32)


def _dot_tn(a, b):
    return lax.dot_general(a, b, (((0,), (0,)), ((), ())), preferred_element_type=F32)


def _inv_rms(x):
    return lax.rsqrt(jnp.mean(x * x, axis=-1, keepdims=True) + EPS)


def _silu(x):
    return x * (1.0 / (1.0 + jnp.exp(-x)))


def _resident(shape):
    nd = len(shape)
    return pl.BlockSpec(shape, lambda *_: (0,) * nd, pipeline_mode=pl.Buffered(1))


def _params(*sem):
    return pltpu.CompilerParams(dimension_semantics=sem, vmem_limit_bytes=VMEM_LIMIT)


def _pre_kernel(x_ref, g_ref, w_ref, o_ref):
    x = x_ref[...]
    xn = (x * _inv_rms(x) * g_ref[...]).astype(BF16)
    o_ref[...] = _dot(xn, w_ref[...])


def _pre(x, g, w, tm):
    t, d = x.shape
    n = w.shape[1]
    return pl.pallas_call(
        _pre_kernel,
        grid=(t // tm,),
        in_specs=[pl.BlockSpec((tm, d), lambda i: (i, 0)),
                  _resident((1, d)),
                  _resident((d, n))],
        out_specs=pl.BlockSpec((tm, n), lambda i: (i, 0)),
        out_shape=jax.ShapeDtypeStruct((t, n), F32),
        compiler_params=_params("parallel"),
        name="pre_proj",
    )(x, g, w)


def _level_ref_rows(g, m):
    c, w = g.shape
    if m >= SUBLANES:
        g3 = g.reshape(c // (2 * m), 2 * m, w)
        r = jnp.broadcast_to(g3[:, m - 1:m, :], g3.shape)
        return r.reshape(c, w)
    g3 = g.reshape(c // SUBLANES, SUBLANES, w)
    rib = lax.broadcasted_iota(jnp.int32, g3.shape, 1)
    out = None
    for start in range(0, SUBLANES, 2 * m):
        r = jnp.broadcast_to(g3[:, start + m - 1:start + m, :], g3.shape)
        out = r if out is None else jnp.where(rib >= start, r, out)
    return out.reshape(c, w)


def _split3(x):
    hi = x.astype(BF16)
    r1 = x - hi.astype(F32)
    mid = r1.astype(BF16)
    lo = (r1 - mid.astype(F32)).astype(BF16)
    return hi, mid, lo


def _hgrn_chunk(qr, fz, iv, og, lb, gon, s_prev):
    c, w = qr.shape
    hg = w // LANES
    head = lambda a, h: a[:, h * LANES:(h + 1) * LANES]
    row = lax.broadcasted_iota(jnp.int32, (c, c), 0)
    col = lax.broadcasted_iota(jnp.int32, (c, c), 1)
    xor = row ^ col
    lower = row > col

    qv = _silu(qr)
    e = jnp.exp(-jnp.abs(fz))
    r = 1.0 / (1.0 + e)
    er = e * r
    pos = fz >= 0.0
    sig = jnp.where(pos, r, er)
    nsig = jnp.where(pos, er, r)
    lf = jnp.log(lb + (1.0 - lb) * sig)
    kk = (1.0 - lb) * nsig

    tri = jnp.where(row >= col, 1.0, 0.0).astype(BF16)
    gcat = _dot(tri, jnp.concatenate(_split3(lf), axis=1))
    g = gcat[:, :w] + gcat[:, w:2 * w] + gcat[:, 2 * w:]

    vb = iv.astype(BF16)
    qg = (qv * jnp.exp(g)).astype(BF16)
    qb = qv.astype(BF16)
    kb = kk.astype(BF16)

    scores = [jnp.where(row == col, _dot_nt(head(qb, h), head(kb, h)), 0.0) for h in range(hg)]
    m = 1
    while m < c:
        em = jnp.exp(-jnp.abs(g - _level_ref_rows(g, m)))
        qe = (qv * em).astype(BF16)
        ke = (kk * em).astype(BF16)
        sel = lower & (xor >= m) & (xor < 2 * m)
        scores = [jnp.where(sel, _dot_nt(head(qe, h), head(ke, h)), scores[h])
                  for h in range(hg)]
        m *= 2
    o = [_dot(head(qg, h), s_prev[h].astype(BF16)) + _dot(scores[h].astype(BF16), head(vb, h))
         for h in range(hg)]

    g_last = g[c - 1:c, :]
    kdec = (kk * jnp.exp(g_last - g)).astype(BF16)
    dh, dm, dl = (p.astype(F32) for p in _split3(jnp.exp(g_last)))
    rid = lax.broadcasted_iota(jnp.int32, (SUBLANES, w), 0)
    d3 = jnp.where(rid == 0, dh, jnp.where(rid == 1, dm, jnp.where(rid == 2, dl, 0.0)))
    d3 = d3.astype(BF16)
    ones = jnp.ones((SUBLANES, LANES), BF16)
    s_new = [_dot_tn(head(d3, h), ones) * s_prev[h] + _dot_tn(head(kdec, h), head(vb, h))
             for h in range(hg)]

    on = jnp.concatenate([o[h] * _inv_rms(o[h]) for h in range(hg)], axis=1)
    return on * gon * _silu(og), s_new


def _hgrn_kernel(*refs, c, tb, hg, has_s0):
    if has_s0:
        q_ref, fz_ref, iv_ref, og_ref, lb_ref, gon_ref, s0_ref, o_ref, so_ref, s_scr = refs
    else:
        q_ref, fz_ref, iv_ref, og_ref, lb_ref, gon_ref, o_ref, so_ref, s_scr = refs
    ci = pl.program_id(2)

    @pl.when(ci == 0)
    def _init():
        if has_s0:
            s_scr[...] = s0_ref[0]
        else:
            s_scr[...] = jnp.zeros_like(s_scr)

    def chunk(rows):
        on, s_new = _hgrn_chunk(q_ref[0, rows, :], fz_ref[0, rows, :], iv_ref[0, rows, :],
                                og_ref[0, rows, :], lb_ref[...], gon_ref[...],
                                [s_scr[h] for h in range(hg)])
        o_ref[0, rows, :] = on.astype(o_ref.dtype)
        for h in range(hg):
            s_scr[h] = s_new[h]

    if tb == c:
        chunk(slice(0, c))
    else:
        def body(j, carry):
            chunk(pl.ds(pl.multiple_of(j * c, c), c))
            return carry
        lax.fori_loop(0, tb // c, body, 0)

    @pl.when(ci == pl.num_programs(2) - 1)
    def _fin():
        so_ref[0] = s_scr[...]


def _hgrn(proj, lb, gon, s0, *, c, tb, hg, out_dtype):
    b, l, _ = proj.shape
    nhg = H_A // hg
    w = hg * LANES

    def tok_spec(k):
        return pl.BlockSpec((1, tb, w), lambda bi, hi, ci, k=k: (bi, ci, k * nhg + hi))

    vec_spec = pl.BlockSpec((1, w), lambda bi, hi, ci: (0, hi))
    st_spec = pl.BlockSpec((1, hg, DK_A, DK_A), lambda bi, hi, ci: (bi, hi, 0, 0))
    in_specs = [tok_spec(0), tok_spec(1), tok_spec(2), tok_spec(3), vec_spec, vec_spec]
    args = [proj, proj, proj, proj, lb, gon]
    if s0 is not None:
        in_specs.append(st_spec)
        args.append(s0)
    return pl.pallas_call(
        functools.partial(_hgrn_kernel, c=c, tb=tb, hg=hg, has_s0=s0 is not None),
        grid=(b, nhg, l // tb),
        in_specs=in_specs,
        out_specs=[pl.BlockSpec((1, tb, w), lambda bi, hi, ci: (bi, ci, hi)), st_spec],
        out_shape=[jax.ShapeDtypeStruct((b, l, D_MODEL), out_dtype),
                   jax.ShapeDtypeStruct((b, H_A, DK_A, DK_A), F32)],
        scratch_shapes=[pltpu.VMEM((hg, DK_A, DK_A), F32)],
        compiler_params=_params("parallel", "parallel", "arbitrary"),
        name="hgrn",
    )(*args)


def _mix_mlp(x_ref, o_ref, wo_ref, gm_ref, wu_ref, wd_ref):
    h1 = x_ref[...] + _dot(o_ref[...].astype(BF16), wo_ref[...])
    xn = (h1 * _inv_rms(h1) * gm_ref[...]).astype(BF16)
    h2 = h1
    for f in range(D_FF // D_MODEL):
        cols = slice(f * D_MODEL, (f + 1) * D_MODEL)
        u = jnp.maximum(_dot(xn, wu_ref[:, cols]), 0.0)
        h2 = h2 + _dot((u * u).astype(BF16), wd_ref[cols, :])
    return h2


def _post0_kernel(x_ref, o_ref, wo_ref, gm_ref, wu_ref, wd_ref, gkv_ref, wkv_ref, gq_ref,
                  wq_ref, h_ref, k_ref, v_ref, q_ref, *, kv_t):
    h2 = _mix_mlp(x_ref, o_ref, wo_ref, gm_ref, wu_ref, wd_ref)
    h_ref[...] = h2
    hn = h2 * _inv_rms(h2)
    hkv = (hn * gkv_ref[...]).astype(BF16)
    if kv_t:
        kvt = _dot_nt(wkv_ref[...], hkv)
        k_ref[0] = kvt[:D_MODEL, :]
        v_ref[0] = kvt[D_MODEL:, :]
    else:
        kv = _dot(hkv, wkv_ref[...])
        k_ref[...] = kv[:, :D_MODEL]
        v_ref[...] = kv[:, D_MODEL:]
    q_ref[...] = _dot((hn * gq_ref[...]).astype(BF16), wq_ref[...]) * (HD_B ** -0.5 * LOG2E)


def _post1_kernel(x_ref, o_ref, wo_ref, gm_ref, wu_ref, wd_ref, gf_ref, y_ref):
    h2 = _mix_mlp(x_ref, o_ref, wo_ref, gm_ref, wu_ref, wd_ref)
    y_ref[...] = h2 * _inv_rms(h2) * gf_ref[...]


def _tok_spec(tm, n):
    return pl.BlockSpec((tm, n), lambda i: (i, 0))


def _post0(x, o, wo, gm, wu, wd, gkv, wkv, gq, wq, tm, seq_t=None):
    t, d = x.shape
    weights = [wo, gm, wu, wd, gkv, wkv, gq, wq]
    kv_spec, kv_shape = _tok_spec(tm, d), jax.ShapeDtypeStruct((t, d), F32)
    if seq_t is not None:
        b, l = seq_t
        nb = l // tm
        kv_spec = pl.BlockSpec((1, d, tm), lambda i: (i // nb, 0, i % nb))
        kv_shape = jax.ShapeDtypeStruct((b, d, l), F32)
    return pl.pallas_call(
        functools.partial(_post0_kernel, kv_t=seq_t is not None),
        grid=(t // tm,),
        in_specs=[_tok_spec(tm, d), _tok_spec(tm, d)] + [_resident(a.shape) for a in weights],
        out_specs=[_tok_spec(tm, d), kv_spec, kv_spec, _tok_spec(tm, d)],
        out_shape=[jax.ShapeDtypeStruct((t, d), F32), kv_shape, kv_shape,
                   jax.ShapeDtypeStruct((t, d), F32)],
        compiler_params=_params("parallel"),
        name="post0",
    )(x, o, *weights)


def _post1(x, o, wo, gm, wu, wd, gf, tm):
    t, d = x.shape
    weights = [wo, gm, wu, wd, gf]
    return pl.pallas_call(
        _post1_kernel,
        grid=(t // tm,),
        in_specs=[_tok_spec(tm, d), _tok_spec(tm, d)] + [_resident(a.shape) for a in weights],
        out_specs=_tok_spec(tm, d),
        out_shape=jax.ShapeDtypeStruct((t, d), F32),
        compiler_params=_params("parallel"),
        name="post1",
    )(x, o, *weights)


def _sb_group(y_all, bias, carry, tt, masks):
    u = y_all.shape[1] // LANES
    parts = [None] * u
    for i in reversed(range(u)):
        y = y_all[:, i * LANES:(i + 1) * LANES] + bias
        v = jnp.maximum(y, 0.0) + jnp.log2(1.0 + jnp.exp2(-jnp.abs(y)))
        if masks is not None:
            v = jnp.where(masks[i], v, 0.0)
        hi = v.astype(BF16)
        lo = (v - hi.astype(F32)).astype(BF16)
        r = _dot(jnp.concatenate([hi, lo], axis=1), tt)
        a = jnp.exp2(y - r[:, :LANES] - carry)
        if masks is not None:
            a = jnp.where(masks[i], a, 0.0)
        parts[i] = a.astype(BF16)
        carry = carry + r[:, LANES:]
    return jnp.concatenate(parts, axis=1), carry


def _attn_kernel(bias_ref, q_ref, k_ref, v_ref, tt_ref, o_ref, kb, vb, acc, carry, *, tq):
    hp = pl.program_id(1)
    qi = pl.program_id(2)
    nsub = tq // LANES
    assert nsub == 2

    @pl.when(qi == 0)
    def _cast():
        for j in range(kb.shape[0]):
            kb[j] = k_ref[0, :, j * LANES:(j + 1) * LANES].astype(BF16)
            vb[j] = v_ref[0, :, j * LANES:(j + 1) * LANES].astype(BF16)

    q = q_ref[0]
    lane = lax.broadcasted_iota(jnp.int32, (tq, LANES), 1)
    first = lane < HD_B
    q2 = jnp.concatenate([jnp.where(first, q, 0.0), jnp.where(first, 0.0, q)], axis=0).astype(BF16)
    rows2 = lax.broadcasted_iota(jnp.int32, (2 * tq, LANES), 0)
    bias2 = jnp.where(rows2 < tq, bias_ref[2 * hp], bias_ref[2 * hp + 1]) * LOG2E
    tt = tt_ref[...]

    def group(j0, u, masks):
        kcat = jnp.concatenate([kb[j0 + i] for i in range(u)], axis=1)
        a, c_new = _sb_group(_dot(q2, kcat), bias2, carry[...], tt, masks)
        carry[...] = c_new
        vcat = jnp.concatenate([vb[j0 + i] for i in range(u)], axis=1)
        return _dot_nt(a, vcat)

    carry[...] = jnp.zeros_like(carry)
    t_idx = jnp.where(rows2 < tq, rows2, rows2 - tq)
    s_idx = lax.broadcasted_iota(jnp.int32, (2 * tq, LANES), 1)
    acc[...] = group(qi * nsub, nsub, [s_idx + i * LANES < t_idx for i in range(nsub)])

    @pl.when(qi % 2 == 1)
    def _pair():
        acc[...] += group(qi * nsub - 2, 2, None)

    top = (qi // 2) * 4

    def body(g, c):
        acc[...] += group(top - 4 * (g + 1), 4, None)
        return c
    lax.fori_loop(0, qi // 2, body, 0)

    o_ref[0] = jnp.where(first, acc[:tq, :], acc[tq:, :]).astype(o_ref.dtype)


def _tri_table():
    j = jnp.arange(LANES)[:, None]
    s = jnp.arange(LANES)[None, :]
    t = jnp.concatenate([(j >= s).astype(BF16), jnp.ones((LANES, LANES), BF16)], axis=1)
    return jnp.concatenate([t, t], axis=0)


def _attn_prompt(q, kt, vt, bias, tq):
    b, l, d = q.shape
    nhp = d // LANES
    kv_spec = pl.BlockSpec((1, LANES, l), lambda bi, hp, qi: (bi, hp, 0))
    return pl.pallas_call(
        functools.partial(_attn_kernel, tq=tq),
        grid=(b, nhp, l // tq),
        in_specs=[pl.BlockSpec(memory_space=pltpu.SMEM),
                  pl.BlockSpec((1, tq, LANES), lambda bi, hp, qi: (bi, qi, hp)),
                  kv_spec, kv_spec,
                  _resident((2 * LANES, 2 * LANES))],
        out_specs=pl.BlockSpec((1, tq, LANES), lambda bi, hp, qi: (bi, qi, hp)),
        out_shape=jax.ShapeDtypeStruct((b, l, d), BF16),
        scratch_shapes=[pltpu.VMEM((l // LANES, LANES, LANES), BF16),
                        pltpu.VMEM((l // LANES, LANES, LANES), BF16),
                        pltpu.VMEM((2 * tq, LANES), F32), pltpu.VMEM((2 * tq, LANES), F32)],
        compiler_params=_params("parallel", "parallel", "arbitrary"),
        name="attn_prompt",
    )(bias, q, kt, vt, _tri_table())


def _sattn_kernel(pt_ref, q_ref, kn_ref, vn_ref, *rest, lq, npp):
    kp_refs, vp_refs = rest[:npp], rest[npp:2 * npp]
    bias_ref, tt_ref, o_ref, qbd, acc_new, acc_t, carry = rest[2 * npp:]
    si = pl.program_id(1)
    rows = H_B * lq
    row = lax.broadcasted_iota(jnp.int32, (rows, D_MODEL), 0)
    col = lax.broadcasted_iota(jnp.int32, (rows, D_MODEL), 1)
    own = (col // HD_B) == (row // lq)
    tt = tt_ref[...]

    @pl.when(si == 0)
    def _new_keys():
        q16 = jnp.concatenate([q_ref[0]] * H_B, axis=0)
        qbd[...] = jnp.where(own, q16, 0.0).astype(BF16)
        pad = jnp.zeros((PAGE - lq, D_MODEL), F32)
        kblk = jnp.concatenate([kn_ref[0], pad], axis=0).astype(BF16)
        vblk = jnp.concatenate([vn_ref[0], pad], axis=0).astype(BF16)
        t_idx = lax.broadcasted_iota(jnp.int32, (rows, LANES), 0) % lq
        s_idx = lax.broadcasted_iota(jnp.int32, (rows, LANES), 1)
        a, c_new = _sb_group(_dot_nt(qbd[...], kblk), bias_ref[...],
                             jnp.zeros((rows, LANES), F32), tt, [s_idx < t_idx])
        carry[...] = c_new
        acc_new[...] = _dot(a, vblk)
        acc_t[...] = jnp.zeros_like(acc_t)

    @pl.when(si > 0)
    def _pages():
        kcat = jnp.concatenate([kp_refs[i][0].astype(BF16) for i in reversed(range(npp))], axis=1)
        a, c_new = _sb_group(_dot(qbd[...], kcat), bias_ref[...], carry[...], tt, None)
        carry[...] = c_new
        vcat = jnp.concatenate([vp_refs[i][0].astype(BF16) for i in reversed(range(npp))], axis=1)
        acc_t[...] += _dot_nt(vcat, a)

    @pl.when(si == pl.num_programs(1) - 1)
    def _fin():
        total = acc_new[...] + acc_t[...].T
        sel = jnp.where(own, total, 0.0).reshape(H_B, lq, D_MODEL)
        o_ref[0] = jnp.sum(sel, axis=0)


def _attn_sample(q, kn, vn, cache_kt, cache_vt, page_table, bias, npp):
    b, lq, d = q.shape
    n_pages = page_table.shape[1]
    rows = H_B * lq
    bias_rows = jnp.broadcast_to(jnp.repeat(bias * LOG2E, lq)[:, None], (rows, LANES))

    def page_spec(i):
        def page_map(bi, si, pt):
            return (pt[bi, n_pages - 1 - jnp.maximum(si - 1, 0) * npp - i], 0, 0)
        return pl.BlockSpec((1, d, PAGE), page_map)

    new_spec = pl.BlockSpec((1, lq, d), lambda bi, si, pt: (bi, 0, 0))
    pages = [page_spec(i) for i in range(npp)]
    grid_spec = pltpu.PrefetchScalarGridSpec(
        num_scalar_prefetch=1,
        grid=(b, n_pages // npp + 1),
        in_specs=[new_spec, new_spec, new_spec] + pages + pages + [
            pl.BlockSpec((rows, LANES), lambda bi, si, pt: (0, 0)),
            pl.BlockSpec((2 * LANES, 2 * LANES), lambda bi, si, pt: (0, 0))],
        out_specs=new_spec,
        scratch_shapes=[pltpu.VMEM((rows, d), BF16), pltpu.VMEM((rows, d), F32),
                        pltpu.VMEM((d, rows), F32), pltpu.VMEM((rows, LANES), F32)],
    )
    return pl.pallas_call(
        functools.partial(_sattn_kernel, lq=lq, npp=npp),
        grid_spec=grid_spec,
        out_shape=jax.ShapeDtypeStruct((b, lq, d), F32),
        compiler_params=_params("parallel", "arbitrary"),
        name="attn_sample",
    )(page_table, q, kn, vn, *([cache_kt] * npp), *([cache_vt] * npp), bias_rows, _tri_table())


def _trunk(x, s0, past, w, *, hgrn_cfg):
    b, l, d = x.shape
    xt = x.reshape(b * l, d)
    proj = _pre(xt, w["g_mix0"], w["w_in"], PRE_TM)
    o, s_new = _hgrn(proj.reshape(b, l, 4 * d), w["lb"], w["g_onorm"], s0, **hgrn_cfg)
    post0 = functools.partial(_post0, xt, o.reshape(b * l, d), w["w_out_a"], w["g_mlp0"],
                              w["w_up0"], w["w_down0"], w["g_kv"])
    if past is None:
        h, kt, vt, q = post0(w["w_kv_t"], w["g_mix1"], w["w_q"], POST0_TM, seq_t=(b, l))
        att = _attn_prompt(q.reshape(b, l, d), kt, vt, w["sb_bias"], ATTN_TQ)
        k4, v4 = (jnp.transpose(a.reshape(b, H_B, HD_B, l), (0, 3, 1, 2)) for a in (kt, vt))
    else:
        cache_kt, cache_vt, page_table = past
        h, k, v, q = post0(w["w_kv"], w["g_mix1"], w["w_q"], POST0_TM)
        k3, v3, q3 = (a.reshape(b, l, d) for a in (k, v, q))
        att = _attn_sample(q3, k3, v3, cache_kt, cache_vt, page_table, w["sb_bias"],
                           PAGES_PER_STEP)
        k4, v4 = k.reshape(b, l, H_B, HD_B), v.reshape(b, l, H_B, HD_B)
    y = _post1(h, att.reshape(b * l, d), w["w_out_b"], w["g_mlp1"], w["w_up1"], w["w_down1"],
               w["g_final"], POST1_TM)
    return y.reshape(b, l, d), s_new[None], k4, v4


def kernel(x_prompt, x_sample, state_hgrn, cache_k, cache_v, page_table, lb_logits, w_in_a,
           g_onorm_a, w_out_a, w_kv, g_kv, w_q_b, w_out_b, sb_bias, w_up, w_down, g_mix,
           g_mlp, g_final):
    lb_all = jnp.cumsum(jax.nn.softmax(lb_logits.astype(F32), axis=0), axis=0)
    row = lambda a: a.reshape(1, -1).astype(F32)
    w = {
        "lb": row(lb_all[0]), "g_mix0": row(g_mix[0]), "g_mix1": row(g_mix[1]),
        "g_mlp0": row(g_mlp[0]), "g_mlp1": row(g_mlp[1]), "g_kv": row(g_kv),
        "g_final": row(g_final), "g_onorm": row(g_onorm_a[0]),
        "sb_bias": sb_bias[0].astype(F32),
        "w_in": w_in_a[0].astype(BF16), "w_out_a": w_out_a[0].astype(BF16),
        "w_kv": w_kv.astype(BF16), "w_kv_t": w_kv.T.astype(BF16), "w_q": w_q_b[0].astype(BF16),
        "w_out_b": w_out_b[0].astype(BF16),
        "w_up0": w_up[0].astype(BF16), "w_up1": w_up[1].astype(BF16),
        "w_down0": w_down[0].astype(BF16), "w_down1": w_down[1].astype(BF16),
    }
    n_pool = cache_k.shape[0]
    page_t = lambda a: jnp.transpose(a, (0, 2, 3, 1)).reshape(n_pool, D_MODEL, PAGE)
    past = (page_t(cache_k), page_t(cache_v), page_table)
    ls = x_sample.shape[1]
    y_p, s_p, k_p, v_p = _trunk(x_prompt, None, None, w,
                                hgrn_cfg=dict(out_dtype=BF16, **HGRN_PROMPT))
    y_s, s_s, k_s, v_s = _trunk(x_sample, state_hgrn[0], past, w,
                                hgrn_cfg=dict(c=ls, tb=ls, hg=H_A, out_dtype=F32))
    return (y_p, y_s, s_p, s_s, k_p, v_p, k_s, v_s)
```

```python
import functools
import math

import jax
import jax.numpy as jnp
from jax import lax
from jax.experimental import pallas as pl
from jax.experimental.pallas import tpu as pltpu

F32 = jnp.float32
BF16 = jnp.bfloat16

D_MODEL = 1024
D_FF = 4 * D_MODEL
H_A = 8
DK_A = 128
H_B = 16
HD_B = 64
EPS = 1e-6
PAGE = 128
LANES = 128
SUBLANES = 8
VMEM_LIMIT = 56 * 1024 * 1024
LOG2E = math.log2(math.e)
EXP2_CLAMP = 126.0

PRE_TM = 512
POST0_TM = 256
POST1_TM = 512
HGRN_PROMPT = dict(c=64, tb=256, hg=8)
ATTN_TQ = 256
ATTN_STREAMS = 2
PAGES_PER_STEP = 16


def _dot(a, b):
    return jnp.dot(a, b, preferred_element_type=F32)


def _dot_nt(a, b):
    return lax.dot_general(a, b, (((1,), (1,)), ((), ())), preferred_element_type=F32)


def _dot_tn(a, b):
    return lax.dot_general(a, b, (((0,), (0,)), ((), ())), preferred_element_type=F32)


def _inv_rms(x):
    return lax.rsqrt(jnp.mean(x * x, axis=-1, keepdims=True) + EPS)


def _silu(x):
    return x * (1.0 / (1.0 + jnp.exp(-x)))


def _resident(shape):
    nd = len(shape)
    return pl.BlockSpec(shape, lambda *_: (0,) * nd, pipeline_mode=pl.Buffered(1))


def _params(*sem):
    return pltpu.CompilerParams(dimension_semantics=sem, vmem_limit_bytes=VMEM_LIMIT)


def _pre_kernel(x_ref, g_ref, w_ref, o_ref):
    x = x_ref[...]
    xn = (x * _inv_rms(x) * g_ref[...]).astype(BF16)
    o_ref[...] = _dot(xn, w_ref[...])


def _pre(x, g, w, tm):
    t, d = x.shape
    n = w.shape[1]
    return pl.pallas_call(
        _pre_kernel,
        grid=(t // tm,),
        in_specs=[pl.BlockSpec((tm, d), lambda i: (i, 0)),
                  _resident((1, d)),
                  _resident((d, n))],
        out_specs=pl.BlockSpec((tm, n), lambda i: (i, 0)),
        out_shape=jax.ShapeDtypeStruct((t, n), F32),
        compiler_params=_params("parallel"),
        name="pre_proj",
    )(x, g, w)


def _level_ref_rows(g, m):
    c, w = g.shape
    if m >= SUBLANES:
        g3 = g.reshape(c // (2 * m), 2 * m, w)
        r = jnp.broadcast_to(g3[:, m - 1:m, :], g3.shape)
        return r.reshape(c, w)
    g3 = g.reshape(c // SUBLANES, SUBLANES, w)
    rib = lax.broadcasted_iota(jnp.int32, g3.shape, 1)
    out = None
    for start in range(0, SUBLANES, 2 * m):
        r = jnp.broadcast_to(g3[:, start + m - 1:start + m, :], g3.shape)
        out = r if out is None else jnp.where(rib >= start, r, out)
    return out.reshape(c, w)


def _split3(x):
    hi = x.astype(BF16)
    r1 = x - hi.astype(F32)
    mid = r1.astype(BF16)
    lo = (r1 - mid.astype(F32)).astype(BF16)
    return hi, mid, lo


def _hgrn_chunk(qr, fz, iv, og, lb, gon, s_prev):
    c, w = qr.shape
    hg = w // LANES
    head = lambda a, h: a[:, h * LANES:(h + 1) * LANES]
    row = lax.broadcasted_iota(jnp.int32, (c, c), 0)
    col = lax.broadcasted_iota(jnp.int32, (c, c), 1)
    xor = row ^ col
    lower = row > col

    qv = _silu(qr)
    e = jnp.exp(-jnp.abs(fz))
    r = 1.0 / (1.0 + e)
    er = e * r
    pos = fz >= 0.0
    sig = jnp.where(pos, r, er)
    nsig = jnp.where(pos, er, r)
    lf = jnp.log(lb + (1.0 - lb) * sig)
    kk = (1.0 - lb) * nsig

    tri = jnp.where(row >= col, 1.0, 0.0).astype(BF16)
    gcat = _dot(tri, jnp.concatenate(_split3(lf), axis=1))
    g = gcat[:, :w] + gcat[:, w:2 * w] + gcat[:, 2 * w:]

    vb = iv.astype(BF16)
    qg = (qv * jnp.exp(g)).astype(BF16)
    qb = qv.astype(BF16)
    kb = kk.astype(BF16)

    scores = [jnp.where(row == col, _dot_nt(head(qb, h), head(kb, h)), 0.0) for h in range(hg)]
    m = 1
    while m < c:
        em = jnp.exp(-jnp.abs(g - _level_ref_rows(g, m)))
        qe = (qv * em).astype(BF16)
        ke = (kk * em).astype(BF16)
        sel = lower & (xor >= m) & (xor < 2 * m)
        scores = [jnp.where(sel, _dot_nt(head(qe, h), head(ke, h)), scores[h])
                  for h in range(hg)]
        m *= 2
    o = [_dot(head(qg, h), s_prev[h].astype(BF16)) + _dot(scores[h].astype(BF16), head(vb, h))
         for h in range(hg)]

    g_last = g[c - 1:c, :]
    kdec = (kk * jnp.exp(g_last - g)).astype(BF16)
    dh, dm, dl = (p.astype(F32) for p in _split3(jnp.exp(g_last)))
    rid = lax.broadcasted_iota(jnp.int32, (SUBLANES, w), 0)
    d3 = jnp.where(rid == 0, dh, jnp.where(rid == 1, dm, jnp.where(rid == 2, dl, 0.0)))
    d3 = d3.astype(BF16)
    ones = jnp.ones((SUBLANES, LANES), BF16)
    s_new = [_dot_tn(head(d3, h), ones) * s_prev[h] + _dot_tn(head(kdec, h), head(vb, h))
             for h in range(hg)]

    on = jnp.concatenate([o[h] * _inv_rms(o[h]) for h in range(hg)], axis=1)
    return on * gon * _silu(og), s_new


def _hgrn_kernel(*refs, c, tb, hg, has_s0):
    if has_s0:
        q_ref, fz_ref, iv_ref, og_ref, lb_ref, gon_ref, s0_ref, o_ref, so_ref, s_scr = refs
    else:
        q_ref, fz_ref, iv_ref, og_ref, lb_ref, gon_ref, o_ref, so_ref, s_scr = refs
    ci = pl.program_id(2)

    @pl.when(ci == 0)
    def _init():
        if has_s0:
            s_scr[...] = s0_ref[0]
        else:
            s_scr[...] = jnp.zeros_like(s_scr)

    def chunk(rows):
        on, s_new = _hgrn_chunk(q_ref[0, rows, :], fz_ref[0, rows, :], iv_ref[0, rows, :],
                                og_ref[0, rows, :], lb_ref[...], gon_ref[...],
                                [s_scr[h] for h in range(hg)])
        o_ref[0, rows, :] = on.astype(o_ref.dtype)
        for h in range(hg):
            s_scr[h] = s_new[h]

    if tb == c:
        chunk(slice(0, c))
    else:
        def body(j, carry):
            chunk(pl.ds(pl.multiple_of(j * c, c), c))
            return carry
        lax.fori_loop(0, tb // c, body, 0)

    @pl.when(ci == pl.num_programs(2) - 1)
    def _fin():
        so_ref[0] = s_scr[...]


def _hgrn(proj, lb, gon, s0, *, c, tb, hg, out_dtype):
    b, l, _ = proj.shape
    nhg = H_A // hg
    w = hg * LANES

    def tok_spec(k):
        return pl.BlockSpec((1, tb, w), lambda bi, hi, ci, k=k: (bi, ci, k * nhg + hi))

    vec_spec = pl.BlockSpec((1, w), lambda bi, hi, ci: (0, hi))
    st_spec = pl.BlockSpec((1, hg, DK_A, DK_A), lambda bi, hi, ci: (bi, hi, 0, 0))
    in_specs = [tok_spec(0), tok_spec(1), tok_spec(2), tok_spec(3), vec_spec, vec_spec]
    args = [proj, proj, proj, proj, lb, gon]
    if s0 is not None:
        in_specs.append(st_spec)
        args.append(s0)
    return pl.pallas_call(
        functools.partial(_hgrn_kernel, c=c, tb=tb, hg=hg, has_s0=s0 is not None),
        grid=(b, nhg, l // tb),
        in_specs=in_specs,
        out_specs=[pl.BlockSpec((1, tb, w), lambda bi, hi, ci: (bi, ci, hi)), st_spec],
        out_shape=[jax.ShapeDtypeStruct((b, l, D_MODEL), out_dtype),
                   jax.ShapeDtypeStruct((b, H_A, DK_A, DK_A), F32)],
        scratch_shapes=[pltpu.VMEM((hg, DK_A, DK_A), F32)],
        compiler_params=_params("parallel", "parallel", "arbitrary"),
        name="hgrn",
    )(*args)


def _mix_mlp(x_ref, o_ref, wo_ref, gm_ref, wu_ref, wd_ref):
    h1 = x_ref[...] + _dot(o_ref[...].astype(BF16), wo_ref[...])
    xn = (h1 * _inv_rms(h1) * gm_ref[...]).astype(BF16)
    h2 = h1
    for f in range(D_FF // D_MODEL):
        cols = slice(f * D_MODEL, (f + 1) * D_MODEL)
        u = jnp.maximum(_dot(xn, wu_ref[:, cols]), 0.0)
        h2 = h2 + _dot((u * u).astype(BF16), wd_ref[cols, :])
    return h2


def _post0_kernel(x_ref, o_ref, wo_ref, gm_ref, wu_ref, wd_ref, gkv_ref, wkv_ref, gq_ref,
                  wq_ref, h_ref, k_ref, v_ref, q_ref, *, kv_t):
    h2 = _mix_mlp(x_ref, o_ref, wo_ref, gm_ref, wu_ref, wd_ref)
    h_ref[...] = h2
    hn = h2 * _inv_rms(h2)
    hkv = (hn * gkv_ref[...]).astype(BF16)
    if kv_t:
        kvt = _dot_nt(wkv_ref[...], hkv)
        k_ref[0] = kvt[:D_MODEL, :]
        v_ref[0] = kvt[D_MODEL:, :]
    else:
        kv = _dot(hkv, wkv_ref[...])
        k_ref[...] = kv[:, :D_MODEL]
        v_ref[...] = kv[:, D_MODEL:]
    q_ref[...] = _dot((hn * gq_ref[...]).astype(BF16), wq_ref[...]) * (HD_B ** -0.5 * LOG2E)


def _post1_kernel(x_ref, o_ref, wo_ref, gm_ref, wu_ref, wd_ref, gf_ref, y_ref):
    h2 = _mix_mlp(x_ref, o_ref, wo_ref, gm_ref, wu_ref, wd_ref)
    y_ref[...] = h2 * _inv_rms(h2) * gf_ref[...]


def _tok_spec(tm, n):
    return pl.BlockSpec((tm, n), lambda i: (i, 0))


def _post0(x, o, wo, gm, wu, wd, gkv, wkv, gq, wq, tm, seq_t=None):
    t, d = x.shape
    weights = [wo, gm, wu, wd, gkv, wkv, gq, wq]
    kv_spec, kv_shape = _tok_spec(tm, d), jax.ShapeDtypeStruct((t, d), F32)
    if seq_t is not None:
        b, l = seq_t
        nb = l // tm
        kv_spec = pl.BlockSpec((1, d, tm), lambda i: (i // nb, 0, i % nb))
        kv_shape = jax.ShapeDtypeStruct((b, d, l), F32)
    return pl.pallas_call(
        functools.partial(_post0_kernel, kv_t=seq_t is not None),
        grid=(t // tm,),
        in_specs=[_tok_spec(tm, d), _tok_spec(tm, d)] + [_resident(a.shape) for a in weights],
        out_specs=[_tok_spec(tm, d), kv_spec, kv_spec, _tok_spec(tm, d)],
        out_shape=[jax.ShapeDtypeStruct((t, d), F32), kv_shape, kv_shape,
                   jax.ShapeDtypeStruct((t, d), F32)],
        compiler_params=_params("parallel"),
        name="post0",
    )(x, o, *weights)


def _post1(x, o, wo, gm, wu, wd, gf, tm):
    t, d = x.shape
    weights = [wo, gm, wu, wd, gf]
    return pl.pallas_call(
        _post1_kernel,
        grid=(t // tm,),
        in_specs=[_tok_spec(tm, d), _tok_spec(tm, d)] + [_resident(a.shape) for a in weights],
        out_specs=_tok_spec(tm, d),
        out_shape=jax.ShapeDtypeStruct((t, d), F32),
        compiler_params=_params("parallel"),
        name="post1",
    )(x, o, *weights)


def _sb_group(ys, biases, carries, tt, masks):
    n = len(ys)
    u = ys[0].shape[1] // LANES
    carries = list(carries)
    parts = [[None] * u for _ in range(n)]

    def softplus2(y, mask):
        v = jnp.maximum(y, jnp.log2(1.0 + jnp.exp2(jnp.minimum(y, EXP2_CLAMP))))
        v = v if mask is None else jnp.where(mask, v, 0.0)
        return v.astype(BF16)

    for lo in reversed(range(0, u, 2)):
        blocks = [lo, lo + 1] if lo + 1 < u else [lo]
        y = [[ys[k][:, i * LANES:(i + 1) * LANES] + biases[k] for i in blocks] for k in range(n)]
        v = [[softplus2(y[k][b], None if masks is None else masks[i])
              for b, i in enumerate(blocks)] for k in range(n)]
        if len(blocks) == 1:
            v = [vk + [jnp.zeros_like(vk[0])] for vk in v]
        r = [_dot(jnp.concatenate(vk, axis=1), tt) for vk in v]
        for k in range(n):
            for b, i in enumerate(blocks):
                a = jnp.exp2(y[k][b] - r[k][:, b * LANES:(b + 1) * LANES] - carries[k])
                if masks is not None:
                    a = jnp.where(masks[i], a, 0.0)
                parts[k][i] = a.astype(BF16)
            carries[k] = carries[k] + jnp.broadcast_to(r[k][:, 0:1], carries[k].shape)
    return [jnp.concatenate(p, axis=1) for p in parts], carries


def _attn_kernel(bias_ref, q_ref, k_ref, v_ref, tt_ref, o_ref, kb, vb, acc, carry, *, tq, ns):
    hg = pl.program_id(1)
    qi = pl.program_id(2)
    nsub = tq // LANES
    assert nsub == 2
    nblk = kb.shape[0] // ns

    @pl.when(qi == 0)
    def _cast():
        for s in range(ns):
            feat = slice(s * LANES, (s + 1) * LANES)
            for j in range(nblk):
                keys = slice(j * LANES, (j + 1) * LANES)
                kb[s * nblk + j] = k_ref[0, feat, keys].astype(BF16)
                vb[s * nblk + j] = v_ref[0, feat, keys].astype(BF16)

    lane = lax.broadcasted_iota(jnp.int32, (tq, LANES), 1)
    first = lane < HD_B
    rows2 = lax.broadcasted_iota(jnp.int32, (2 * tq, LANES), 0)
    q2, bias2 = [], []
    for s in range(ns):
        q = q_ref[0, :, s * LANES:(s + 1) * LANES]
        q2.append(jnp.concatenate([jnp.where(first, q, 0.0), jnp.where(first, 0.0, q)],
                                  axis=0).astype(BF16))
        hp = hg * ns + s
        bias2.append(jnp.where(rows2 < tq, bias_ref[2 * hp], bias_ref[2 * hp + 1]) * LOG2E)
    tt = tt_ref[...]

    def group(j0, u, masks):
        blocks = lambda ref, s: jnp.concatenate([ref[s * nblk + j0 + i] for i in range(u)], axis=1)
        ys = [_dot(q2[s], blocks(kb, s)) for s in range(ns)]
        a, c_new = _sb_group(ys, bias2, [carry[s] for s in range(ns)], tt, masks)
        for s in range(ns):
            carry[s] = c_new[s]
        return [_dot_nt(a[s], blocks(vb, s)) for s in range(ns)]

    def accumulate(res):
        for s in range(ns):
            acc[s] += res[s]

    carry[...] = jnp.zeros_like(carry)
    t_idx = jnp.where(rows2 < tq, rows2, rows2 - tq)
    s_idx = lax.broadcasted_iota(jnp.int32, (2 * tq, LANES), 1)
    res = group(qi * nsub, nsub, [s_idx + i * LANES < t_idx for i in range(nsub)])
    for s in range(ns):
        acc[s] = res[s]

    @pl.when(qi % 2 == 1)
    def _pair():
        accumulate(group(qi * nsub - 2, 2, None))

    top = (qi // 2) * 4

    def body(g, c):
        accumulate(group(top - 4 * (g + 1), 4, None))
        return c
    lax.fori_loop(0, qi // 2, body, 0)

    o_ref[0] = jnp.concatenate(
        [jnp.where(first, acc[s, :tq, :], acc[s, tq:, :]) for s in range(ns)],
        axis=1).astype(o_ref.dtype)


def _tri_table():
    j = jnp.arange(LANES)[:, None]
    s = jnp.arange(LANES)[None, :]
    t = (j >= s).astype(BF16)
    top = jnp.concatenate([t, jnp.zeros((LANES, LANES), BF16)], axis=1)
    bottom = jnp.concatenate([jnp.ones((LANES, LANES), BF16), t], axis=1)
    return jnp.concatenate([top, bottom], axis=0)


def _attn_prompt(q, kt, vt, bias, tq, ns):
    b, l, d = q.shape
    w = ns * LANES
    nblk = l // LANES
    kv_spec = pl.BlockSpec((1, w, l), lambda bi, hg, qi: (bi, hg, 0))
    q_spec = pl.BlockSpec((1, tq, w), lambda bi, hg, qi: (bi, qi, hg))
    return pl.pallas_call(
        functools.partial(_attn_kernel, tq=tq, ns=ns),
        grid=(b, d // w, l // tq),
        in_specs=[pl.BlockSpec(memory_space=pltpu.SMEM), q_spec, kv_spec, kv_spec,
                  _resident((2 * LANES, 2 * LANES))],
        out_specs=q_spec,
        out_shape=jax.ShapeDtypeStruct((b, l, d), BF16),
        scratch_shapes=[pltpu.VMEM((ns * nblk, LANES, LANES), BF16),
                        pltpu.VMEM((ns * nblk, LANES, LANES), BF16),
                        pltpu.VMEM((ns, 2 * tq, LANES), F32),
                        pltpu.VMEM((ns, 2 * tq, LANES), F32)],
        compiler_params=_params("parallel", "parallel", "arbitrary"),
        name="attn_prompt",
    )(bias, q, kt, vt, _tri_table())


def _sattn_kernel(pt_ref, q_ref, kn_ref, vn_ref, *rest, lq, npp):
    kp_refs, vp_refs = rest[:npp], rest[npp:2 * npp]
    bias_ref, tt_ref, o_ref, qbd, acc_new, acc_t, carry = rest[2 * npp:]
    si = pl.program_id(1)
    rows = H_B * lq
    row = lax.broadcasted_iota(jnp.int32, (rows, D_MODEL), 0)
    col = lax.broadcasted_iota(jnp.int32, (rows, D_MODEL), 1)
    own = (col // HD_B) == (row // lq)
    tt = tt_ref[...]

    @pl.when(si == 0)
    def _new_keys():
        q16 = jnp.concatenate([q_ref[0]] * H_B, axis=0)
        qbd[...] = jnp.where(own, q16, 0.0).astype(BF16)
        pad = jnp.zeros((PAGE - lq, D_MODEL), F32)
        kblk = jnp.concatenate([kn_ref[0], pad], axis=0).astype(BF16)
        vblk = jnp.concatenate([vn_ref[0], pad], axis=0).astype(BF16)
        t_idx = lax.broadcasted_iota(jnp.int32, (rows, LANES), 0) % lq
        s_idx = lax.broadcasted_iota(jnp.int32, (rows, LANES), 1)
        (a,), (c_new,) = _sb_group([_dot_nt(qbd[...], kblk)], [bias_ref[...]],
                                   [jnp.zeros((rows, LANES), F32)], tt, [s_idx < t_idx])
        carry[...] = c_new
        acc_new[...] = _dot(a, vblk)
        acc_t[...] = jnp.zeros_like(acc_t)

    kcat = jnp.concatenate([kp_refs[i][0].astype(BF16) for i in reversed(range(npp))], axis=1)
    (a,), (c_new,) = _sb_group([_dot(qbd[...], kcat)], [bias_ref[...]], [carry[...]], tt, None)
    carry[...] = c_new
    vcat = jnp.concatenate([vp_refs[i][0].astype(BF16) for i in reversed(range(npp))], axis=1)
    acc_t[...] += _dot_nt(vcat, a)

    @pl.when(si == pl.num_programs(1) - 1)
    def _fin():
        total = acc_new[...] + acc_t[...].T
        sel = jnp.where(own, total, 0.0).reshape(H_B, lq, D_MODEL)
        o_ref[0] = jnp.sum(sel, axis=0)


def _attn_sample(q, kn, vn, cache_kt, cache_vt, page_table, bias, npp):
    b, lq, d = q.shape
    n_pages = page_table.shape[1]
    rows = H_B * lq
    bias_rows = jnp.broadcast_to(jnp.repeat(bias * LOG2E, lq)[:, None], (rows, LANES))

    def page_spec(i):
        def page_map(bi, si, pt):
            return (pt[bi, n_pages - 1 - si * npp - i], 0, 0)
        return pl.BlockSpec((1, d, PAGE), page_map)

    new_spec = pl.BlockSpec((1, lq, d), lambda bi, si, pt: (bi, 0, 0))
    pages = [page_spec(i) for i in range(npp)]
    grid_spec = pltpu.PrefetchScalarGridSpec(
        num_scalar_prefetch=1,
        grid=(b, n_pages // npp),
        in_specs=[new_spec, new_spec, new_spec] + pages + pages + [
            pl.BlockSpec((rows, LANES), lambda bi, si, pt: (0, 0)),
            pl.BlockSpec((2 * LANES, 2 * LANES), lambda bi, si, pt: (0, 0))],
        out_specs=new_spec,
        scratch_shapes=[pltpu.VMEM((rows, d), BF16), pltpu.VMEM((rows, d), F32),
                        pltpu.VMEM((d, rows), F32), pltpu.VMEM((rows, LANES), F32)],
    )
    return pl.pallas_call(
        functools.partial(_sattn_kernel, lq=lq, npp=npp),
        grid_spec=grid_spec,
        out_shape=jax.ShapeDtypeStruct((b, lq, d), F32),
        compiler_params=_params("parallel", "arbitrary"),
        name="attn_sample",
    )(page_table, q, kn, vn, *([cache_kt] * npp), *([cache_vt] * npp), bias_rows, _tri_table())


def _trunk(x, s0, past, w, *, hgrn_cfg):
    b, l, d = x.shape
    xt = x.reshape(b * l, d)
    proj = _pre(xt, w["g_mix0"], w["w_in"], PRE_TM)
    o, s_new = _hgrn(proj.reshape(b, l, 4 * d), w["lb"], w["g_onorm"], s0, **hgrn_cfg)
    post0 = functools.partial(_post0, xt, o.reshape(b * l, d), w["w_out_a"], w["g_mlp0"],
                              w["w_up0"], w["w_down0"], w["g_kv"])
    if past is None:
        h, kt, vt, q = post0(w["w_kv_t"], w["g_mix1"], w["w_q"], POST0_TM, seq_t=(b, l))
        att = _attn_prompt(q.reshape(b, l, d), kt, vt, w["sb_bias"], ATTN_TQ, ATTN_STREAMS)
        k4, v4 = (jnp.transpose(a.reshape(b, H_B, HD_B, l), (0, 3, 1, 2)) for a in (kt, vt))
    else:
        cache_kt, cache_vt, page_table = past
        h, k, v, q = post0(w["w_kv"], w["g_mix1"], w["w_q"], POST0_TM)
        k3, v3, q3 = (a.reshape(b, l, d) for a in (k, v, q))
        att = _attn_sample(q3, k3, v3, cache_kt, cache_vt, page_table, w["sb_bias"],
                           PAGES_PER_STEP)
        k4, v4 = k.reshape(b, l, H_B, HD_B), v.reshape(b, l, H_B, HD_B)
    y = _post1(h, att.reshape(b * l, d), w["w_out_b"], w["g_mlp1"], w["w_up1"], w["w_down1"],
               w["g_final"], POST1_TM)
    return y.reshape(b, l, d), s_new[None], k4, v4


def kernel(x_prompt, x_sample, state_hgrn, cache_k, cache_v, page_table, lb_logits, w_in_a,
           g_onorm_a, w_out_a, w_kv, g_kv, w_q_b, w_out_b, sb_bias, w_up, w_down, g_mix,
           g_mlp, g_final):
    lb_all = jnp.cumsum(jax.nn.softmax(lb_logits.astype(F32), axis=0), axis=0)
    row = lambda a: a.reshape(1, -1).astype(F32)
    w = {
        "lb": row(lb_all[0]), "g_mix0": row(g_mix[0]), "g_mix1": row(g_mix[1]),
        "g_mlp0": row(g_mlp[0]), "g_mlp1": row(g_mlp[1]), "g_kv": row(g_kv),
        "g_final": row(g_final), "g_onorm": row(g_onorm_a[0]),
        "sb_bias": sb_bias[0].astype(F32),
        "w_in": w_in_a[0].astype(BF16), "w_out_a": w_out_a[0].astype(BF16),
        "w_kv": w_kv.astype(BF16), "w_kv_t": w_kv.T.astype(BF16), "w_q": w_q_b[0].astype(BF16),
        "w_out_b": w_out_b[0].astype(BF16),
        "w_up0": w_up[0].astype(BF16), "w_up1": w_up[1].astype(BF16),
        "w_down0": w_down[0].astype(BF16), "w_down1": w_down[1].astype(BF16),
    }
    n_pool = cache_k.shape[0]
    page_t = lambda a: jnp.transpose(a, (0, 2, 3, 1)).reshape(n_pool, D_MODEL, PAGE)
    past = (page_t(cache_k), page_t(cache_v), page_table)
    ls = x_sample.shape[1]
    y_p, s_p, k_p, v_p = _trunk(x_prompt, None, None, w,
                                hgrn_cfg=dict(out_dtype=BF16, **HGRN_PROMPT))
    y_s, s_s, k_s, v_s = _trunk(x_sample, state_hgrn[0], past, w,
                                hgrn_cfg=dict(c=ls, tb=ls, hg=H_A, out_dtype=F32))
    return (y_p, y_s, s_p, s_s, k_p, v_p, k_s, v_s)
```

```python
import functools
import math

import jax
import jax.numpy as jnp
from jax import lax
from jax.experimental import pallas as pl
from jax.experimental.pallas import tpu as pltpu

F32 = jnp.float32
BF16 = jnp.bfloat16

D_MODEL = 1024
D_FF = 4 * D_MODEL
H_A = 8
DK_A = 128
H_B = 16
HD_B = 64
EPS = 1e-6
PAGE = 128
LANES = 128
SUBLANES = 8
VMEM_LIMIT = 56 * 1024 * 1024
LOG2E = math.log2(math.e)
EXP2_CLAMP = 126.0

PRE_TM = 512
POST0_TM = 256
POST1_TM = 512
HGRN_PROMPT = dict(c=64, tb=256, hg=8, bb=1)
HGRN_SAMPLE_SEQS = 4
ATTN_TQ = 256
ATTN_STREAMS = 4
PAGES_PER_STEP = 16


def _dot(a, b):
    return jnp.dot(a, b, preferred_element_type=F32)


def _dot_nt(a, b):
    return lax.dot_general(a, b, (((1,), (1,)), ((), ())), preferred_element_type=F32)


def _dot_tn(a, b):
    return lax.dot_general(a, b, (((0,), (0,)), ((), ())), preferred_element_type=F32)


def _inv_rms(x):
    return lax.rsqrt(jnp.mean(x * x, axis=-1, keepdims=True) + EPS)


def _silu(x):
    return x * (1.0 / (1.0 + jnp.exp(-x)))


def _resident(shape):
    nd = len(shape)
    return pl.BlockSpec(shape, lambda *_: (0,) * nd, pipeline_mode=pl.Buffered(1))


def _params(*sem):
    return pltpu.CompilerParams(dimension_semantics=sem, vmem_limit_bytes=VMEM_LIMIT)


def _pre_kernel(x_ref, g_ref, w_ref, o_ref):
    x = x_ref[...]
    xn = (x * _inv_rms(x) * g_ref[...]).astype(BF16)
    o_ref[...] = _dot(xn, w_ref[...])


def _pre(x, g, w, tm):
    t, d = x.shape
    n = w.shape[1]
    return pl.pallas_call(
        _pre_kernel,
        grid=(t // tm,),
        in_specs=[pl.BlockSpec((tm, d), lambda i: (i, 0)),
                  _resident((1, d)),
                  _resident((d, n))],
        out_specs=pl.BlockSpec((tm, n), lambda i: (i, 0)),
        out_shape=jax.ShapeDtypeStruct((t, n), F32),
        compiler_params=_params("parallel"),
        name="pre_proj",
    )(x, g, w)


def _level_ref_rows(g, m):
    c, w = g.shape
    if m >= SUBLANES:
        g3 = g.reshape(c // (2 * m), 2 * m, w)
        r = jnp.broadcast_to(g3[:, m - 1:m, :], g3.shape)
        return r.reshape(c, w)
    g3 = g.reshape(c // SUBLANES, SUBLANES, w)
    rib = lax.broadcasted_iota(jnp.int32, g3.shape, 1)
    out = None
    for start in range(0, SUBLANES, 2 * m):
        r = jnp.broadcast_to(g3[:, start + m - 1:start + m, :], g3.shape)
        out = r if out is None else jnp.where(rib >= start, r, out)
    return out.reshape(c, w)


def _split3(x):
    hi = x.astype(BF16)
    r1 = x - hi.astype(F32)
    mid = r1.astype(BF16)
    lo = (r1 - mid.astype(F32)).astype(BF16)
    return hi, mid, lo


def _hgrn_chunk(qr, fz, iv, og, lb, gon, s_prev):
    c, w = qr.shape
    hg = w // LANES
    head = lambda a, h: a[:, h * LANES:(h + 1) * LANES]
    row = lax.broadcasted_iota(jnp.int32, (c, c), 0)
    col = lax.broadcasted_iota(jnp.int32, (c, c), 1)
    xor = row ^ col
    lower = row > col

    qv = _silu(qr)
    e = jnp.exp(-jnp.abs(fz))
    r = 1.0 / (1.0 + e)
    er = e * r
    pos = fz >= 0.0
    sig = jnp.where(pos, r, er)
    nsig = jnp.where(pos, er, r)
    lf = jnp.log(lb + (1.0 - lb) * sig)
    kk = (1.0 - lb) * nsig

    tri = jnp.where(row >= col, 1.0, 0.0).astype(BF16)
    gcat = _dot(tri, jnp.concatenate(_split3(lf), axis=1))
    g = gcat[:, :w] + gcat[:, w:2 * w] + gcat[:, 2 * w:]

    vb = iv.astype(BF16)
    qg = (qv * jnp.exp(g)).astype(BF16)
    qb = qv.astype(BF16)
    kb = kk.astype(BF16)

    scores = [jnp.where(row == col, _dot_nt(head(qb, h), head(kb, h)), 0.0) for h in range(hg)]
    m = 1
    while m < c:
        em = jnp.exp(-jnp.abs(g - _level_ref_rows(g, m)))
        qe = (qv * em).astype(BF16)
        ke = (kk * em).astype(BF16)
        sel = lower & (xor >= m) & (xor < 2 * m)
        scores = [jnp.where(sel, _dot_nt(head(qe, h), head(ke, h)), scores[h])
                  for h in range(hg)]
        m *= 2
    o = [_dot(head(qg, h), s_prev[h].astype(BF16)) + _dot(scores[h].astype(BF16), head(vb, h))
         for h in range(hg)]

    g_last = g[c - 1:c, :]
    kdec = (kk * jnp.exp(g_last - g)).astype(BF16)
    dh, dm, dl = (p.astype(F32) for p in _split3(jnp.exp(g_last)))
    rid = lax.broadcasted_iota(jnp.int32, (SUBLANES, w), 0)
    d3 = jnp.where(rid == 0, dh, jnp.where(rid == 1, dm, jnp.where(rid == 2, dl, 0.0)))
    d3 = d3.astype(BF16)
    ones = jnp.ones((SUBLANES, LANES), BF16)
    s_new = [_dot_tn(head(d3, h), ones) * s_prev[h] + _dot_tn(head(kdec, h), head(vb, h))
             for h in range(hg)]

    on = jnp.concatenate([o[h] * _inv_rms(o[h]) for h in range(hg)], axis=1)
    return on * gon * _silu(og), s_new


def _hgrn_kernel(*refs, c, tb, hg, bb, has_s0):
    if has_s0:
        q_ref, fz_ref, iv_ref, og_ref, lb_ref, gon_ref, s0_ref, o_ref, so_ref, s_scr = refs
    else:
        q_ref, fz_ref, iv_ref, og_ref, lb_ref, gon_ref, o_ref, so_ref, s_scr = refs
    ci = pl.program_id(2)
    w = hg * LANES

    @pl.when(ci == 0)
    def _init():
        if has_s0:
            s_scr[...] = s0_ref[...]
        else:
            s_scr[...] = jnp.zeros_like(s_scr)

    wide = lambda ref, rows: jnp.concatenate([ref[bi, rows, :] for bi in range(bb)], axis=1)
    lb = jnp.concatenate([lb_ref[...]] * bb, axis=1)
    gon = jnp.concatenate([gon_ref[...]] * bb, axis=1)

    def chunk(rows):
        on, s_new = _hgrn_chunk(wide(q_ref, rows), wide(fz_ref, rows), wide(iv_ref, rows),
                                wide(og_ref, rows), lb, gon,
                                [s_scr[bi, h] for bi in range(bb) for h in range(hg)])
        for bi in range(bb):
            o_ref[bi, rows, :] = on[:, bi * w:(bi + 1) * w].astype(o_ref.dtype)
            for h in range(hg):
                s_scr[bi, h] = s_new[bi * hg + h]

    if tb == c:
        chunk(slice(0, c))
    else:
        def body(j, carry):
            chunk(pl.ds(pl.multiple_of(j * c, c), c))
            return carry
        lax.fori_loop(0, tb // c, body, 0)

    @pl.when(ci == pl.num_programs(2) - 1)
    def _fin():
        so_ref[...] = s_scr[...]


def _hgrn(proj, lb, gon, s0, *, c, tb, hg, bb, out_dtype):
    b, l, _ = proj.shape
    nhg = H_A // hg
    w = hg * LANES

    def tok_spec(k):
        return pl.BlockSpec((bb, tb, w), lambda bi, hi, ci, k=k: (bi, ci, k * nhg + hi))

    vec_spec = pl.BlockSpec((1, w), lambda bi, hi, ci: (0, hi))
    st_spec = pl.BlockSpec((bb, hg, DK_A, DK_A), lambda bi, hi, ci: (bi, hi, 0, 0))
    in_specs = [tok_spec(0), tok_spec(1), tok_spec(2), tok_spec(3), vec_spec, vec_spec]
    args = [proj, proj, proj, proj, lb, gon]
    if s0 is not None:
        in_specs.append(st_spec)
        args.append(s0)
    return pl.pallas_call(
        functools.partial(_hgrn_kernel, c=c, tb=tb, hg=hg, bb=bb, has_s0=s0 is not None),
        grid=(b // bb, nhg, l // tb),
        in_specs=in_specs,
        out_specs=[pl.BlockSpec((bb, tb, w), lambda bi, hi, ci: (bi, ci, hi)), st_spec],
        out_shape=[jax.ShapeDtypeStruct((b, l, D_MODEL), out_dtype),
                   jax.ShapeDtypeStruct((b, H_A, DK_A, DK_A), F32)],
        scratch_shapes=[pltpu.VMEM((bb, hg, DK_A, DK_A), F32)],
        compiler_params=_params("parallel", "parallel", "arbitrary"),
        name="hgrn",
    )(*args)


def _mix_mlp(x_ref, o_ref, wo_ref, gm_ref, wu_ref, wd_ref):
    h1 = x_ref[...] + _dot(o_ref[...].astype(BF16), wo_ref[...])
    xn = (h1 * _inv_rms(h1) * gm_ref[...]).astype(BF16)
    h2 = h1
    for f in range(D_FF // D_MODEL):
        cols = slice(f * D_MODEL, (f + 1) * D_MODEL)
        u = jnp.maximum(_dot(xn, wu_ref[:, cols]), 0.0)
        h2 = h2 + _dot((u * u).astype(BF16), wd_ref[cols, :])
    return h2


def _post0_kernel(x_ref, o_ref, wo_ref, gm_ref, wu_ref, wd_ref, gkv_ref, wkv_ref, gq_ref,
                  wq_ref, h_ref, k_ref, v_ref, q_ref, *, kv_t):
    h2 = _mix_mlp(x_ref, o_ref, wo_ref, gm_ref, wu_ref, wd_ref)
    h_ref[...] = h2
    hn = h2 * _inv_rms(h2)
    hkv = (hn * gkv_ref[...]).astype(BF16)
    if kv_t:
        kvt = _dot_nt(wkv_ref[...], hkv)
        k_ref[0] = kvt[:D_MODEL, :]
        v_ref[0] = kvt[D_MODEL:, :]
    else:
        kv = _dot(hkv, wkv_ref[...])
        k_ref[...] = kv[:, :D_MODEL]
        v_ref[...] = kv[:, D_MODEL:]
    q_ref[...] = _dot((hn * gq_ref[...]).astype(BF16), wq_ref[...]) * (HD_B ** -0.5 * LOG2E)


def _post1_kernel(x_ref, o_ref, wo_ref, gm_ref, wu_ref, wd_ref, gf_ref, y_ref):
    h2 = _mix_mlp(x_ref, o_ref, wo_ref, gm_ref, wu_ref, wd_ref)
    y_ref[...] = h2 * _inv_rms(h2) * gf_ref[...]


def _tok_spec(tm, n):
    return pl.BlockSpec((tm, n), lambda i: (i, 0))


def _post0(x, o, wo, gm, wu, wd, gkv, wkv, gq, wq, tm, seq_t=None):
    t, d = x.shape
    weights = [wo, gm, wu, wd, gkv, wkv, gq, wq]
    kv_spec, kv_shape = _tok_spec(tm, d), jax.ShapeDtypeStruct((t, d), F32)
    if seq_t is not None:
        b, l = seq_t
        nb = l // tm
        kv_spec = pl.BlockSpec((1, d, tm), lambda i: (i // nb, 0, i % nb))
        kv_shape = jax.ShapeDtypeStruct((b, d, l), F32)
    return pl.pallas_call(
        functools.partial(_post0_kernel, kv_t=seq_t is not None),
        grid=(t // tm,),
        in_specs=[_tok_spec(tm, d), _tok_spec(tm, d)] + [_resident(a.shape) for a in weights],
        out_specs=[_tok_spec(tm, d), kv_spec, kv_spec, _tok_spec(tm, d)],
        out_shape=[jax.ShapeDtypeStruct((t, d), F32), kv_shape, kv_shape,
                   jax.ShapeDtypeStruct((t, d), F32)],
        compiler_params=_params("parallel"),
        name="post0",
    )(x, o, *weights)


def _post1(x, o, wo, gm, wu, wd, gf, tm):
    t, d = x.shape
    weights = [wo, gm, wu, wd, gf]
    return pl.pallas_call(
        _post1_kernel,
        grid=(t // tm,),
        in_specs=[_tok_spec(tm, d), _tok_spec(tm, d)] + [_resident(a.shape) for a in weights],
        out_specs=_tok_spec(tm, d),
        out_shape=jax.ShapeDtypeStruct((t, d), F32),
        compiler_params=_params("parallel"),
        name="post1",
    )(x, o, *weights)


def _sb_group(ys, biases, carries, tt, masks):
    n = len(ys)
    u = ys[0].shape[1] // LANES
    carries = list(carries)
    parts = [[None] * u for _ in range(n)]

    def softplus2(y, mask):
        v = jnp.maximum(y, jnp.log2(1.0 + jnp.exp2(jnp.minimum(y, EXP2_CLAMP))))
        v = v if mask is None else jnp.where(mask, v, 0.0)
        return v.astype(BF16)

    for lo in reversed(range(0, u, 2)):
        blocks = [lo, lo + 1] if lo + 1 < u else [lo]
        y = [[ys[k][:, i * LANES:(i + 1) * LANES] + biases[k] for i in blocks] for k in range(n)]
        v = [[softplus2(y[k][b], None if masks is None else masks[i])
              for b, i in enumerate(blocks)] for k in range(n)]
        if len(blocks) == 1:
            v = [vk + [jnp.zeros_like(vk[0])] for vk in v]
        r = [_dot(jnp.concatenate(vk, axis=1), tt) for vk in v]
        for k in range(n):
            for b, i in enumerate(blocks):
                a = jnp.exp2(y[k][b] - r[k][:, b * LANES:(b + 1) * LANES] - carries[k])
                if masks is not None:
                    a = jnp.where(masks[i], a, 0.0)
                parts[k][i] = a.astype(BF16)
            carries[k] = carries[k] + jnp.broadcast_to(r[k][:, 0:1], carries[k].shape)
    return [jnp.concatenate(p, axis=1) for p in parts], carries


def _attn_kernel(bias_ref, q_ref, k_ref, v_ref, tt_ref, o_ref, kb, vb, acc, carry, *, tq, ns):
    hg = pl.program_id(1)
    qi = pl.program_id(2)
    nsub = tq // LANES
    assert nsub == 2
    nblk = kb.shape[0] // ns

    @pl.when(qi == 0)
    def _cast():
        for s in range(ns):
            feat = slice(s * LANES, (s + 1) * LANES)
            for j in range(nblk):
                keys = slice(j * LANES, (j + 1) * LANES)
                kb[s * nblk + j] = k_ref[0, feat, keys].astype(BF16)
                vb[s * nblk + j] = v_ref[0, feat, keys].astype(BF16)

    lane = lax.broadcasted_iota(jnp.int32, (tq, LANES), 1)
    first = lane < HD_B
    rows2 = lax.broadcasted_iota(jnp.int32, (2 * tq, LANES), 0)
    q2, bias2 = [], []
    for s in range(ns):
        q = q_ref[0, :, s * LANES:(s + 1) * LANES]
        q2.append(jnp.concatenate([jnp.where(first, q, 0.0), jnp.where(first, 0.0, q)],
                                  axis=0).astype(BF16))
        hp = hg * ns + s
        bias2.append(jnp.where(rows2 < tq, bias_ref[2 * hp], bias_ref[2 * hp + 1]) * LOG2E)
    tt = tt_ref[...]

    def group(j0, u, masks):
        blocks = lambda ref, s: jnp.concatenate([ref[s * nblk + j0 + i] for i in range(u)], axis=1)
        ys = [_dot(q2[s], blocks(kb, s)) for s in range(ns)]
        a, c_new = _sb_group(ys, bias2, [carry[s] for s in range(ns)], tt, masks)
        for s in range(ns):
            carry[s] = c_new[s]
        return [_dot_nt(a[s], blocks(vb, s)) for s in range(ns)]

    def accumulate(res):
        for s in range(ns):
            acc[s] += res[s]

    carry[...] = jnp.zeros_like(carry)
    t_idx = jnp.where(rows2 < tq, rows2, rows2 - tq)
    s_idx = lax.broadcasted_iota(jnp.int32, (2 * tq, LANES), 1)
    res = group(qi * nsub, nsub, [s_idx + i * LANES < t_idx for i in range(nsub)])
    for s in range(ns):
        acc[s] = res[s]

    @pl.when(qi % 2 == 1)
    def _pair():
        accumulate(group(qi * nsub - 2, 2, None))

    top = (qi // 2) * 4

    def body(g, c):
        accumulate(group(top - 4 * (g + 1), 4, None))
        return c
    lax.fori_loop(0, qi // 2, body, 0)

    o_ref[0] = jnp.concatenate(
        [jnp.where(first, acc[s, :tq, :], acc[s, tq:, :]) for s in range(ns)],
        axis=1).astype(o_ref.dtype)


def _tri_table():
    j = jnp.arange(LANES)[:, None]
    s = jnp.arange(LANES)[None, :]
    t = (j >= s).astype(BF16)
    top = jnp.concatenate([t, jnp.zeros((LANES, LANES), BF16)], axis=1)
    bottom = jnp.concatenate([jnp.ones((LANES, LANES), BF16), t], axis=1)
    return jnp.concatenate([top, bottom], axis=0)


def _attn_prompt(q, kt, vt, bias, tq, ns):
    b, l, d = q.shape
    w = ns * LANES
    nblk = l // LANES
    kv_spec = pl.BlockSpec((1, w, l), lambda bi, hg, qi: (bi, hg, 0))
    q_spec = pl.BlockSpec((1, tq, w), lambda bi, hg, qi: (bi, qi, hg))
    return pl.pallas_call(
        functools.partial(_attn_kernel, tq=tq, ns=ns),
        grid=(b, d // w, l // tq),
        in_specs=[pl.BlockSpec(memory_space=pltpu.SMEM), q_spec, kv_spec, kv_spec,
                  _resident((2 * LANES, 2 * LANES))],
        out_specs=q_spec,
        out_shape=jax.ShapeDtypeStruct((b, l, d), BF16),
        scratch_shapes=[pltpu.VMEM((ns * nblk, LANES, LANES), BF16),
                        pltpu.VMEM((ns * nblk, LANES, LANES), BF16),
                        pltpu.VMEM((ns, 2 * tq, LANES), F32),
                        pltpu.VMEM((ns, 2 * tq, LANES), F32)],
        compiler_params=_params("parallel", "parallel", "arbitrary"),
        name="attn_prompt",
    )(bias, q, kt, vt, _tri_table())


def _sattn_kernel(pt_ref, q_ref, kn_ref, vn_ref, *rest, lq, npp):
    kp_refs, vp_refs = rest[:npp], rest[npp:2 * npp]
    bias_ref, tt_ref, o_ref, qbd, acc_new, acc_t, carry = rest[2 * npp:]
    si = pl.program_id(1)
    rows = H_B * lq
    row = lax.broadcasted_iota(jnp.int32, (rows, D_MODEL), 0)
    col = lax.broadcasted_iota(jnp.int32, (rows, D_MODEL), 1)
    own = (col // HD_B) == (row // lq)
    tt = tt_ref[...]

    @pl.when(si == 0)
    def _new_keys():
        q16 = jnp.concatenate([q_ref[0]] * H_B, axis=0)
        qbd[...] = jnp.where(own, q16, 0.0).astype(BF16)
        pad = jnp.zeros((PAGE - lq, D_MODEL), F32)
        kblk = jnp.concatenate([kn_ref[0], pad], axis=0).astype(BF16)
        vblk = jnp.concatenate([vn_ref[0], pad], axis=0).astype(BF16)
        t_idx = lax.broadcasted_iota(jnp.int32, (rows, LANES), 0) % lq
        s_idx = lax.broadcasted_iota(jnp.int32, (rows, LANES), 1)
        (a,), (c_new,) = _sb_group([_dot_nt(qbd[...], kblk)], [bias_ref[...]],
                                   [jnp.zeros((rows, LANES), F32)], tt, [s_idx < t_idx])
        carry[...] = c_new
        acc_new[...] = _dot(a, vblk)
        acc_t[...] = jnp.zeros_like(acc_t)

    kcat = jnp.concatenate([kp_refs[i][0].astype(BF16) for i in reversed(range(npp))], axis=1)
    (a,), (c_new,) = _sb_group([_dot(qbd[...], kcat)], [bias_ref[...]], [carry[...]], tt, None)
    carry[...] = c_new
    vcat = jnp.concatenate([vp_refs[i][0].astype(BF16) for i in reversed(range(npp))], axis=1)
    acc_t[...] += _dot_nt(vcat, a)

    @pl.when(si == pl.num_programs(1) - 1)
    def _fin():
        total = acc_new[...] + acc_t[...].T
        sel = jnp.where(own, total, 0.0).reshape(H_B, lq, D_MODEL)
        o_ref[0] = jnp.sum(sel, axis=0)


def _attn_sample(q, kn, vn, cache_kt, cache_vt, page_table, bias, npp):
    b, lq, d = q.shape
    n_pages = page_table.shape[1]
    rows = H_B * lq
    bias_rows = jnp.broadcast_to(jnp.repeat(bias * LOG2E, lq)[:, None], (rows, LANES))

    def page_spec(i):
        def page_map(bi, si, pt):
            return (pt[bi, n_pages - 1 - si * npp - i], 0, 0)
        return pl.BlockSpec((1, d, PAGE), page_map)

    new_spec = pl.BlockSpec((1, lq, d), lambda bi, si, pt: (bi, 0, 0))
    pages = [page_spec(i) for i in range(npp)]
    grid_spec = pltpu.PrefetchScalarGridSpec(
        num_scalar_prefetch=1,
        grid=(b, n_pages // npp),
        in_specs=[new_spec, new_spec, new_spec] + pages + pages + [
            pl.BlockSpec((rows, LANES), lambda bi, si, pt: (0, 0)),
            pl.BlockSpec((2 * LANES, 2 * LANES), lambda bi, si, pt: (0, 0))],
        out_specs=new_spec,
        scratch_shapes=[pltpu.VMEM((rows, d), BF16), pltpu.VMEM((rows, d), F32),
                        pltpu.VMEM((d, rows), F32), pltpu.VMEM((rows, LANES), F32)],
    )
    return pl.pallas_call(
        functools.partial(_sattn_kernel, lq=lq, npp=npp),
        grid_spec=grid_spec,
        out_shape=jax.ShapeDtypeStruct((b, lq, d), F32),
        compiler_params=_params("parallel", "arbitrary"),
        name="attn_sample",
    )(page_table, q, kn, vn, *([cache_kt] * npp), *([cache_vt] * npp), bias_rows, _tri_table())


def _trunk(x, s0, past, w, *, hgrn_cfg):
    b, l, d = x.shape
    xt = x.reshape(b * l, d)
    proj = _pre(xt, w["g_mix0"], w["w_in"], PRE_TM)
    o, s_new = _hgrn(proj.reshape(b, l, 4 * d), w["lb"], w["g_onorm"], s0, **hgrn_cfg)
    post0 = functools.partial(_post0, xt, o.reshape(b * l, d), w["w_out_a"], w["g_mlp0"],
                              w["w_up0"], w["w_down0"], w["g_kv"])
    if past is None:
        h, kt, vt, q = post0(w["w_kv_t"], w["g_mix1"], w["w_q"], POST0_TM, seq_t=(b, l))
        att = _attn_prompt(q.reshape(b, l, d), kt, vt, w["sb_bias"], ATTN_TQ, ATTN_STREAMS)
        k4, v4 = (jnp.transpose(a.reshape(b, H_B, HD_B, l), (0, 3, 1, 2)) for a in (kt, vt))
    else:
        cache_kt, cache_vt, page_table = past
        h, k, v, q = post0(w["w_kv"], w["g_mix1"], w["w_q"], POST0_TM)
        k3, v3, q3 = (a.reshape(b, l, d) for a in (k, v, q))
        att = _attn_sample(q3, k3, v3, cache_kt, cache_vt, page_table, w["sb_bias"],
                           PAGES_PER_STEP)
        k4, v4 = k.reshape(b, l, H_B, HD_B), v.reshape(b, l, H_B, HD_B)
    y = _post1(h, att.reshape(b * l, d), w["w_out_b"], w["g_mlp1"], w["w_up1"], w["w_down1"],
               w["g_final"], POST1_TM)
    return y.reshape(b, l, d), s_new[None], k4, v4


def kernel(x_prompt, x_sample, state_hgrn, cache_k, cache_v, page_table, lb_logits, w_in_a,
           g_onorm_a, w_out_a, w_kv, g_kv, w_q_b, w_out_b, sb_bias, w_up, w_down, g_mix,
           g_mlp, g_final):
    lb_all = jnp.cumsum(jax.nn.softmax(lb_logits.astype(F32), axis=0), axis=0)
    row = lambda a: a.reshape(1, -1).astype(F32)
    w = {
        "lb": row(lb_all[0]), "g_mix0": row(g_mix[0]), "g_mix1": row(g_mix[1]),
        "g_mlp0": row(g_mlp[0]), "g_mlp1": row(g_mlp[1]), "g_kv": row(g_kv),
        "g_final": row(g_final), "g_onorm": row(g_onorm_a[0]),
        "sb_bias": sb_bias[0].astype(F32),
        "w_in": w_in_a[0].astype(BF16), "w_out_a": w_out_a[0].astype(BF16),
        "w_kv": w_kv.astype(BF16), "w_kv_t": w_kv.T.astype(BF16), "w_q": w_q_b[0].astype(BF16),
        "w_out_b": w_out_b[0].astype(BF16),
        "w_up0": w_up[0].astype(BF16), "w_up1": w_up[1].astype(BF16),
        "w_down0": w_down[0].astype(BF16), "w_down1": w_down[1].astype(BF16),
    }
    n_pool = cache_k.shape[0]
    page_t = lambda a: jnp.transpose(a, (0, 2, 3, 1)).reshape(n_pool, D_MODEL, PAGE)
    past = (page_t(cache_k), page_t(cache_v), page_table)
    ls = x_sample.shape[1]
    y_p, s_p, k_p, v_p = _trunk(x_prompt, None, None, w,
                                hgrn_cfg=dict(out_dtype=BF16, **HGRN_PROMPT))
    y_s, s_s, k_s, v_s = _trunk(x_sample, state_hgrn[0], past, w,
                                hgrn_cfg=dict(c=ls, tb=ls, hg=H_A, bb=HGRN_SAMPLE_SEQS,
                                              out_dtype=F32))
    return (y_p, y_s, s_p, s_s, k_p, v_p, k_s, v_s)
```

```python
import functools
import math

import jax
import jax.numpy as jnp
from jax import lax
from jax.experimental import pallas as pl
from jax.experimental.pallas import tpu as pltpu

F32 = jnp.float32
BF16 = jnp.bfloat16

D_MODEL = 1024
D_FF = 4 * D_MODEL
H_A = 8
DK_A = 128
H_B = 16
HD_B = 64
EPS = 1e-6
PAGE = 128
LANES = 128
SUBLANES = 8
VMEM_LIMIT = 56 * 1024 * 1024
LOG2E = math.log2(math.e)
EXP2_CLAMP = 126.0

PRE_TM = 512
POST0_TM = 256
POST1_TM = 512
HGRN_PROMPT = dict(c=64, tb=128, hg=8, bb=1)
HGRN_SAMPLE_SEQS = 4
ATTN_TQ = 256
ATTN_STREAMS = 4


def _dot(a, b):
    return jnp.dot(a, b, preferred_element_type=F32)


def _dot_nt(a, b):
    return lax.dot_general(a, b, (((1,), (1,)), ((), ())), preferred_element_type=F32)


def _dot_tn(a, b):
    return lax.dot_general(a, b, (((0,), (0,)), ((), ())), preferred_element_type=F32)


def _inv_rms(x):
    return lax.rsqrt(jnp.mean(x * x, axis=-1, keepdims=True) + EPS)


def _silu(x):
    return x * (1.0 / (1.0 + jnp.exp(-x)))


def _resident(shape):
    nd = len(shape)
    return pl.BlockSpec(shape, lambda *_: (0,) * nd, pipeline_mode=pl.Buffered(1))


def _params(*sem):
    return pltpu.CompilerParams(dimension_semantics=sem, vmem_limit_bytes=VMEM_LIMIT)


def _pre_kernel(x_ref, g_ref, w_ref, o_ref):
    x = x_ref[...]
    xn = (x * _inv_rms(x) * g_ref[...]).astype(BF16)
    o_ref[...] = _dot(xn, w_ref[...])


def _pre(x, g, w, tm):
    t, d = x.shape
    n = w.shape[1]
    return pl.pallas_call(
        _pre_kernel,
        grid=(t // tm,),
        in_specs=[pl.BlockSpec((tm, d), lambda i: (i, 0)),
                  _resident((1, d)),
                  _resident((d, n))],
        out_specs=pl.BlockSpec((tm, n), lambda i: (i, 0)),
        out_shape=jax.ShapeDtypeStruct((t, n), F32),
        compiler_params=_params("parallel"),
        name="pre_proj",
    )(x, g, w)


def _level_ref_rows(g, m):
    c, w = g.shape
    if m >= SUBLANES:
        g3 = g.reshape(c // (2 * m), 2 * m, w)
        r = jnp.broadcast_to(g3[:, m - 1:m, :], g3.shape)
        return r.reshape(c, w)
    g3 = g.reshape(c // SUBLANES, SUBLANES, w)
    rib = lax.broadcasted_iota(jnp.int32, g3.shape, 1)
    out = None
    for start in range(0, SUBLANES, 2 * m):
        r = jnp.broadcast_to(g3[:, start + m - 1:start + m, :], g3.shape)
        out = r if out is None else jnp.where(rib >= start, r, out)
    return out.reshape(c, w)


def _split3(x):
    hi = x.astype(BF16)
    r1 = x - hi.astype(F32)
    mid = r1.astype(BF16)
    lo = (r1 - mid.astype(F32)).astype(BF16)
    return hi, mid, lo


def _hgrn_chunk(qr, fz, iv, og, lb, gon, s_prev):
    c, w = qr.shape
    hg = w // LANES
    head = lambda a, h: a[:, h * LANES:(h + 1) * LANES]
    row = lax.broadcasted_iota(jnp.int32, (c, c), 0)
    col = lax.broadcasted_iota(jnp.int32, (c, c), 1)
    xor = row ^ col
    lower = row > col

    qv = _silu(qr)
    e = jnp.exp(-jnp.abs(fz))
    r = 1.0 / (1.0 + e)
    er = e * r
    pos = fz >= 0.0
    sig = jnp.where(pos, r, er)
    nsig = jnp.where(pos, er, r)
    lf = jnp.log(lb + (1.0 - lb) * sig)
    kk = (1.0 - lb) * nsig

    tri = jnp.where(row >= col, 1.0, 0.0).astype(BF16)
    gcat = _dot(tri, jnp.concatenate(_split3(lf), axis=1))
    g = gcat[:, :w] + gcat[:, w:2 * w] + gcat[:, 2 * w:]

    vb = iv.astype(BF16)
    qg = (qv * jnp.exp(g)).astype(BF16)
    qb = qv.astype(BF16)
    kb = kk.astype(BF16)

    scores = [jnp.where(row == col, _dot_nt(head(qb, h), head(kb, h)), 0.0) for h in range(hg)]
    m = 1
    while m < c:
        em = jnp.exp(-jnp.abs(g - _level_ref_rows(g, m)))
        qe = (qv * em).astype(BF16)
        ke = (kk * em).astype(BF16)
        sel = lower & (xor >= m) & (xor < 2 * m)
        scores = [jnp.where(sel, _dot_nt(head(qe, h), head(ke, h)), scores[h])
                  for h in range(hg)]
        m *= 2
    o = [_dot(head(qg, h), s_prev[h].astype(BF16)) + _dot(scores[h].astype(BF16), head(vb, h))
         for h in range(hg)]

    g_last = g[c - 1:c, :]
    kdec = (kk * jnp.exp(g_last - g)).astype(BF16)
    dh, dm, dl = (p.astype(F32) for p in _split3(jnp.exp(g_last)))
    rid = lax.broadcasted_iota(jnp.int32, (SUBLANES, w), 0)
    d3 = jnp.where(rid == 0, dh, jnp.where(rid == 1, dm, jnp.where(rid == 2, dl, 0.0)))
    d3 = d3.astype(BF16)
    ones = jnp.ones((SUBLANES, LANES), BF16)
    s_new = [_dot_tn(head(d3, h), ones) * s_prev[h] + _dot_tn(head(kdec, h), head(vb, h))
             for h in range(hg)]

    on = jnp.concatenate([o[h] * _inv_rms(o[h]) for h in range(hg)], axis=1)
    return on * gon * _silu(og), s_new


def _hgrn_kernel(*refs, c, tb, hg, bb, has_s0):
    if has_s0:
        q_ref, fz_ref, iv_ref, og_ref, lb_ref, gon_ref, s0_ref, o_ref, so_ref, s_scr = refs
    else:
        q_ref, fz_ref, iv_ref, og_ref, lb_ref, gon_ref, o_ref, so_ref, s_scr = refs
    ci = pl.program_id(2)
    w = hg * LANES

    @pl.when(ci == 0)
    def _init():
        if has_s0:
            s_scr[...] = s0_ref[...]
        else:
            s_scr[...] = jnp.zeros_like(s_scr)

    wide = lambda ref, rows: jnp.concatenate([ref[bi, rows, :] for bi in range(bb)], axis=1)
    lb = jnp.concatenate([lb_ref[...]] * bb, axis=1)
    gon = jnp.concatenate([gon_ref[...]] * bb, axis=1)

    def chunk(rows):
        on, s_new = _hgrn_chunk(wide(q_ref, rows), wide(fz_ref, rows), wide(iv_ref, rows),
                                wide(og_ref, rows), lb, gon,
                                [s_scr[bi, h] for bi in range(bb) for h in range(hg)])
        for bi in range(bb):
            o_ref[bi, rows, :] = on[:, bi * w:(bi + 1) * w].astype(o_ref.dtype)
            for h in range(hg):
                s_scr[bi, h] = s_new[bi * hg + h]

    if tb == c:
        chunk(slice(0, c))
    else:
        def body(j, carry):
            chunk(pl.ds(pl.multiple_of(j * c, c), c))
            return carry
        lax.fori_loop(0, tb // c, body, 0)

    @pl.when(ci == pl.num_programs(2) - 1)
    def _fin():
        so_ref[...] = s_scr[...]


def _hgrn_rider_kernel(pt_ref, *refs, n_in, n_rider_in, hgrn_kernel, rider_kernel):
    del pt_ref
    o_ref, so_ref, r_out, s_scr = refs[n_in + n_rider_in:]
    hgrn_kernel(*refs[:n_in], o_ref, so_ref, s_scr)
    rider_kernel(*refs[n_in:n_in + n_rider_in], r_out)


def _hgrn(proj, lb, gon, s0, *, c, tb, hg, bb, out_dtype, make_rider=None):
    b, l, _ = proj.shape
    nhg = H_A // hg
    w = hg * LANES
    grid = (b // bb, nhg, l // tb)

    def tok_spec(k):
        return pl.BlockSpec((bb, tb, w), lambda bi, hi, ci, *_, k=k: (bi, ci, k * nhg + hi))

    vec_spec = pl.BlockSpec((1, w), lambda bi, hi, ci, *_: (0, hi))
    st_spec = pl.BlockSpec((bb, hg, DK_A, DK_A), lambda bi, hi, ci, *_: (bi, hi, 0, 0))
    in_specs = [tok_spec(0), tok_spec(1), tok_spec(2), tok_spec(3), vec_spec, vec_spec]
    args = [proj, proj, proj, proj, lb, gon]
    if s0 is not None:
        in_specs.append(st_spec)
        args.append(s0)
    body = functools.partial(_hgrn_kernel, c=c, tb=tb, hg=hg, bb=bb, has_s0=s0 is not None)
    out_specs = [pl.BlockSpec((bb, tb, w), lambda bi, hi, ci, *_: (bi, ci, hi)), st_spec]
    out_shape = [jax.ShapeDtypeStruct((b, l, D_MODEL), out_dtype),
                 jax.ShapeDtypeStruct((b, H_A, DK_A, DK_A), F32)]
    scratch = [pltpu.VMEM((bb, hg, DK_A, DK_A), F32)]
    params = _params("parallel", "parallel", "arbitrary")
    if make_rider is None:
        return pl.pallas_call(body, grid=grid, in_specs=in_specs, out_specs=out_specs,
                              out_shape=out_shape, scratch_shapes=scratch,
                              compiler_params=params, name="hgrn")(*args)
    rider = make_rider(grid)
    assert rider["steps"] == grid[0] * grid[1] * grid[2]
    grid_spec = pltpu.PrefetchScalarGridSpec(
        num_scalar_prefetch=1, grid=grid,
        in_specs=in_specs + rider["in_specs"],
        out_specs=out_specs + [rider["out_spec"]],
        scratch_shapes=scratch)
    return pl.pallas_call(
        functools.partial(_hgrn_rider_kernel, n_in=len(args), n_rider_in=len(rider["args"]),
                          hgrn_kernel=body, rider_kernel=rider["kernel"]),
        grid_spec=grid_spec,
        out_shape=out_shape + [rider["out_shape"]],
        compiler_params=params,
        name="hgrn_sample_attn",
    )(rider["prefetch"], *args, *rider["args"])


def _mix_mlp(x_ref, o_ref, wo_ref, gm_ref, wu_ref, wd_ref):
    h1 = x_ref[...] + _dot(o_ref[...].astype(BF16), wo_ref[...])
    xn = (h1 * _inv_rms(h1) * gm_ref[...]).astype(BF16)
    h2 = h1
    for f in range(D_FF // D_MODEL):
        cols = slice(f * D_MODEL, (f + 1) * D_MODEL)
        u = jnp.maximum(_dot(xn, wu_ref[:, cols]), 0.0)
        h2 = h2 + _dot((u * u).astype(BF16), wd_ref[cols, :])
    return h2


def _post0_kernel(x_ref, o_ref, wo_ref, gm_ref, wu_ref, wd_ref, gkv_ref, wkv_ref, gq_ref,
                  wq_ref, h_ref, k_ref, v_ref, q_ref, *, kv_t):
    h2 = _mix_mlp(x_ref, o_ref, wo_ref, gm_ref, wu_ref, wd_ref)
    h_ref[...] = h2
    hn = h2 * _inv_rms(h2)
    hkv = (hn * gkv_ref[...]).astype(BF16)
    if kv_t:
        kvt = _dot_nt(wkv_ref[...], hkv)
        k_ref[0] = kvt[:D_MODEL, :]
        v_ref[0] = kvt[D_MODEL:, :]
    else:
        kv = _dot(hkv, wkv_ref[...])
        k_ref[...] = kv[:, :D_MODEL]
        v_ref[...] = kv[:, D_MODEL:]
    q_ref[...] = _dot((hn * gq_ref[...]).astype(BF16), wq_ref[...]) * (HD_B ** -0.5 * LOG2E)


def _post1_kernel(x_ref, o_ref, wo_ref, gm_ref, wu_ref, wd_ref, gf_ref, y_ref):
    h2 = _mix_mlp(x_ref, o_ref, wo_ref, gm_ref, wu_ref, wd_ref)
    y_ref[...] = h2 * _inv_rms(h2) * gf_ref[...]


def _tok_spec(tm, n):
    return pl.BlockSpec((tm, n), lambda i: (i, 0))


def _post0(x, o, wo, gm, wu, wd, gkv, wkv, gq, wq, tm, seq_t=None):
    t, d = x.shape
    weights = [wo, gm, wu, wd, gkv, wkv, gq, wq]
    kv_spec, kv_shape = _tok_spec(tm, d), jax.ShapeDtypeStruct((t, d), F32)
    if seq_t is not None:
        b, l = seq_t
        nb = l // tm
        kv_spec = pl.BlockSpec((1, d, tm), lambda i: (i // nb, 0, i % nb))
        kv_shape = jax.ShapeDtypeStruct((b, d, l), F32)
    return pl.pallas_call(
        functools.partial(_post0_kernel, kv_t=seq_t is not None),
        grid=(t // tm,),
        in_specs=[_tok_spec(tm, d), _tok_spec(tm, d)] + [_resident(a.shape) for a in weights],
        out_specs=[_tok_spec(tm, d), kv_spec, kv_spec, _tok_spec(tm, d)],
        out_shape=[jax.ShapeDtypeStruct((t, d), F32), kv_shape, kv_shape,
                   jax.ShapeDtypeStruct((t, d), F32)],
        compiler_params=_params("parallel"),
        name="post0",
    )(x, o, *weights)


def _post1(x, o, wo, gm, wu, wd, gf, tm):
    t, d = x.shape
    weights = [wo, gm, wu, wd, gf]
    return pl.pallas_call(
        _post1_kernel,
        grid=(t // tm,),
        in_specs=[_tok_spec(tm, d), _tok_spec(tm, d)] + [_resident(a.shape) for a in weights],
        out_specs=_tok_spec(tm, d),
        out_shape=jax.ShapeDtypeStruct((t, d), F32),
        compiler_params=_params("parallel"),
        name="post1",
    )(x, o, *weights)


def _sb_group(ys, biases, carries, tt, masks):
    n = len(ys)
    u = ys[0].shape[1] // LANES
    carries = list(carries)
    parts = [[None] * u for _ in range(n)]

    def softplus2(y, mask):
        v = jnp.maximum(y, jnp.log2(1.0 + jnp.exp2(jnp.minimum(y, EXP2_CLAMP))))
        v = v if mask is None else jnp.where(mask, v, 0.0)
        return v.astype(BF16)

    for lo in reversed(range(0, u, 2)):
        blocks = [lo, lo + 1] if lo + 1 < u else [lo]
        y = [[ys[k][:, i * LANES:(i + 1) * LANES] + biases[k] for i in blocks] for k in range(n)]
        v = [[softplus2(y[k][b], None if masks is None else masks[i])
              for b, i in enumerate(blocks)] for k in range(n)]
        if len(blocks) == 1:
            v = [vk + [jnp.zeros_like(vk[0])] for vk in v]
        r = [_dot(jnp.concatenate(vk, axis=1), tt) for vk in v]
        for k in range(n):
            for b, i in enumerate(blocks):
                a = jnp.exp2(y[k][b] - r[k][:, b * LANES:(b + 1) * LANES] - carries[k])
                if masks is not None:
                    a = jnp.where(masks[i], a, 0.0)
                parts[k][i] = a.astype(BF16)
            carries[k] = carries[k] + jnp.broadcast_to(r[k][:, 0:1], carries[k].shape)
    return [jnp.concatenate(p, axis=1) for p in parts], carries


def _attn_kernel(bias_ref, q_ref, k_ref, v_ref, tt_ref, o_ref, kb, vb, acc, carry, *, tq, ns):
    hg = pl.program_id(1)
    qi = pl.program_id(2)
    nsub = tq // LANES
    assert nsub == 2
    nblk = kb.shape[0] // ns

    @pl.when(qi == 0)
    def _cast():
        for s in range(ns):
            feat = slice(s * LANES, (s + 1) * LANES)
            for j in range(nblk):
                keys = slice(j * LANES, (j + 1) * LANES)
                kb[s * nblk + j] = k_ref[0, feat, keys].astype(BF16)
                vb[s * nblk + j] = v_ref[0, feat, keys].astype(BF16)

    lane = lax.broadcasted_iota(jnp.int32, (tq, LANES), 1)
    first = lane < HD_B
    rows2 = lax.broadcasted_iota(jnp.int32, (2 * tq, LANES), 0)
    q2, bias2 = [], []
    for s in range(ns):
        q = q_ref[0, :, s * LANES:(s + 1) * LANES]
        q2.append(jnp.concatenate([jnp.where(first, q, 0.0), jnp.where(first, 0.0, q)],
                                  axis=0).astype(BF16))
        hp = hg * ns + s
        bias2.append(jnp.where(rows2 < tq, bias_ref[2 * hp], bias_ref[2 * hp + 1]) * LOG2E)
    tt = tt_ref[...]

    def group(j0, u, masks):
        blocks = lambda ref, s: jnp.concatenate([ref[s * nblk + j0 + i] for i in range(u)], axis=1)
        ys = [_dot(q2[s], blocks(kb, s)) for s in range(ns)]
        a, c_new = _sb_group(ys, bias2, [carry[s] for s in range(ns)], tt, masks)
        for s in range(ns):
            carry[s] = c_new[s]
        return [_dot_nt(a[s], blocks(vb, s)) for s in range(ns)]

    def accumulate(res):
        for s in range(ns):
            acc[s] += res[s]

    carry[...] = jnp.zeros_like(carry)
    t_idx = jnp.where(rows2 < tq, rows2, rows2 - tq)
    s_idx = lax.broadcasted_iota(jnp.int32, (2 * tq, LANES), 1)
    res = group(qi * nsub, nsub, [s_idx + i * LANES < t_idx for i in range(nsub)])
    for s in range(ns):
        acc[s] = res[s]

    @pl.when(qi % 2 == 1)
    def _pair():
        accumulate(group(qi * nsub - 2, 2, None))

    top = (qi // 2) * 4

    def body(g, c):
        accumulate(group(top - 4 * (g + 1), 4, None))
        return c
    lax.fori_loop(0, qi // 2, body, 0)

    o_ref[0] = jnp.concatenate(
        [jnp.where(first, acc[s, :tq, :], acc[s, tq:, :]) for s in range(ns)],
        axis=1).astype(o_ref.dtype)


def _tri_table():
    j = jnp.arange(LANES)[:, None]
    s = jnp.arange(LANES)[None, :]
    t = (j >= s).astype(BF16)
    top = jnp.concatenate([t, jnp.zeros((LANES, LANES), BF16)], axis=1)
    bottom = jnp.concatenate([jnp.ones((LANES, LANES), BF16), t], axis=1)
    return jnp.concatenate([top, bottom], axis=0)


def _attn_prompt(q, kt, vt, bias, tq, ns):
    b, l, d = q.shape
    w = ns * LANES
    nblk = l // LANES
    kv_spec = pl.BlockSpec((1, w, l), lambda bi, hg, qi: (bi, hg, 0))
    q_spec = pl.BlockSpec((1, tq, w), lambda bi, hg, qi: (bi, qi, hg))
    return pl.pallas_call(
        functools.partial(_attn_kernel, tq=tq, ns=ns),
        grid=(b, d // w, l // tq),
        in_specs=[pl.BlockSpec(memory_space=pltpu.SMEM), q_spec, kv_spec, kv_spec,
                  _resident((2 * LANES, 2 * LANES))],
        out_specs=q_spec,
        out_shape=jax.ShapeDtypeStruct((b, l, d), BF16),
        scratch_shapes=[pltpu.VMEM((ns * nblk, LANES, LANES), BF16),
                        pltpu.VMEM((ns * nblk, LANES, LANES), BF16),
                        pltpu.VMEM((ns, 2 * tq, LANES), F32),
                        pltpu.VMEM((ns, 2 * tq, LANES), F32)],
        compiler_params=_params("parallel", "parallel", "arbitrary"),
        name="attn_prompt",
    )(bias, q, kt, vt, _tri_table())


def _sattn_body(q_ref, kn_ref, vn_ref, *rest, lq, npp):
    kp_refs, vp_refs = rest[:npp], rest[npp:2 * npp]
    bias_ref, tt_ref, o_ref = rest[2 * npp:]
    rows = H_B * lq
    row = lax.broadcasted_iota(jnp.int32, (rows, D_MODEL), 0)
    col = lax.broadcasted_iota(jnp.int32, (rows, D_MODEL), 1)
    own = (col // HD_B) == (row // lq)
    tt = tt_ref[...]
    bias = bias_ref[...]

    q16 = jnp.concatenate([q_ref[0]] * H_B, axis=0)
    qbd = jnp.where(own, q16, 0.0).astype(BF16)
    pad = jnp.zeros((PAGE - lq, D_MODEL), F32)
    kblk = jnp.concatenate([kn_ref[0], pad], axis=0).astype(BF16)
    vblk = jnp.concatenate([vn_ref[0], pad], axis=0).astype(BF16)
    t_idx = lax.broadcasted_iota(jnp.int32, (rows, LANES), 0) % lq
    s_idx = lax.broadcasted_iota(jnp.int32, (rows, LANES), 1)
    (a,), (carry,) = _sb_group([_dot_nt(qbd, kblk)], [bias],
                               [jnp.zeros((rows, LANES), F32)], tt, [s_idx < t_idx])
    out_new = _dot(a, vblk)

    kcat = jnp.concatenate([kp_refs[i][0].astype(BF16) for i in reversed(range(npp))], axis=1)
    (a,), _ = _sb_group([_dot(qbd, kcat)], [bias], [carry], tt, None)
    vcat = jnp.concatenate([vp_refs[i][0].astype(BF16) for i in reversed(range(npp))], axis=1)
    out_t = _dot_nt(vcat, a)

    sel = jnp.where(own, out_new + out_t.T, 0.0).reshape(H_B, lq, D_MODEL)
    o_ref[0] = jnp.sum(sel, axis=0)


def _sample_attn_rider(q, kn, vn, cache_kt, cache_vt, page_table, bias, seq_of):
    b, lq, d = q.shape
    n_pages = page_table.shape[1]
    rows = H_B * lq
    bias_rows = jnp.broadcast_to(jnp.repeat(bias * LOG2E, lq)[:, None], (rows, LANES))

    def page_spec(i):
        return pl.BlockSpec((1, d, PAGE),
                            lambda *ids: (ids[-1][seq_of(*ids[:-1]), n_pages - 1 - i], 0, 0))

    seq_spec = pl.BlockSpec((1, lq, d), lambda *ids: (seq_of(*ids[:-1]), 0, 0))
    pages = [page_spec(i) for i in range(n_pages)]
    return dict(
        kernel=functools.partial(_sattn_body, lq=lq, npp=n_pages),
        args=[q, kn, vn] + [cache_kt] * n_pages + [cache_vt] * n_pages + [bias_rows, _tri_table()],
        in_specs=[seq_spec, seq_spec, seq_spec] + pages + pages + [
            pl.BlockSpec((rows, LANES), lambda *ids: (0, 0)),
            pl.BlockSpec((2 * LANES, 2 * LANES), lambda *ids: (0, 0))],
        out_spec=seq_spec,
        out_shape=jax.ShapeDtypeStruct((b, lq, d), F32),
        prefetch=page_table,
        steps=b,
    )


def _post0_args(xt, o, w):
    return (xt, o, w["w_out_a"], w["g_mlp0"], w["w_up0"], w["w_down0"], w["g_kv"])


def _post1_call(h, att, w):
    return _post1(h, att, w["w_out_b"], w["g_mlp1"], w["w_up1"], w["w_down1"], w["g_final"],
                  POST1_TM)


def _trunks(x_p, x_s, s0_s, past, w):
    cache_kt, cache_vt, page_table = past
    d = D_MODEL

    bs, ls, _ = x_s.shape
    xs = x_s.reshape(bs * ls, d)
    proj_s = _pre(xs, w["g_mix0"], w["w_in"], PRE_TM)
    o_s, st_s = _hgrn(proj_s.reshape(bs, ls, 4 * d), w["lb"], w["g_onorm"], s0_s,
                      c=ls, tb=ls, hg=H_A, bb=HGRN_SAMPLE_SEQS, out_dtype=F32)
    h_s, k_s, v_s, q_s = _post0(*_post0_args(xs, o_s.reshape(bs * ls, d), w), w["w_kv"],
                                w["g_mix1"], w["w_q"], POST0_TM)
    k3, v3, q3 = (a.reshape(bs, ls, d) for a in (k_s, v_s, q_s))

    def make_rider(grid):
        seq_of = lambda bi, hi, ci: (bi * grid[1] + hi) * grid[2] + ci
        return _sample_attn_rider(q3, k3, v3, cache_kt, cache_vt, page_table, w["sb_bias"],
                                  seq_of)

    bp, lp, _ = x_p.shape
    xp = x_p.reshape(bp * lp, d)
    proj_p = _pre(xp, w["g_mix0"], w["w_in"], PRE_TM)
    o_p, st_p, att_s = _hgrn(proj_p.reshape(bp, lp, 4 * d), w["lb"], w["g_onorm"], None,
                             out_dtype=BF16, make_rider=make_rider, **HGRN_PROMPT)
    h_p, kt, vt, q_p = _post0(*_post0_args(xp, o_p.reshape(bp * lp, d), w), w["w_kv_t"],
                              w["g_mix1"], w["w_q"], POST0_TM, seq_t=(bp, lp))
    att_p = _attn_prompt(q_p.reshape(bp, lp, d), kt, vt, w["sb_bias"], ATTN_TQ, ATTN_STREAMS)
    k_p, v_p = (jnp.transpose(a.reshape(bp, H_B, HD_B, lp), (0, 3, 1, 2)) for a in (kt, vt))

    y_p = _post1_call(h_p, att_p.reshape(bp * lp, d), w).reshape(bp, lp, d)
    y_s = _post1_call(h_s, att_s.reshape(bs * ls, d), w).reshape(bs, ls, d)
    return (y_p, y_s, st_p[None], st_s[None], k_p, v_p,
            k_s.reshape(bs, ls, H_B, HD_B), v_s.reshape(bs, ls, H_B, HD_B))


def kernel(x_prompt, x_sample, state_hgrn, cache_k, cache_v, page_table, lb_logits, w_in_a,
           g_onorm_a, w_out_a, w_kv, g_kv, w_q_b, w_out_b, sb_bias, w_up, w_down, g_mix,
           g_mlp, g_final):
    lb_all = jnp.cumsum(jax.nn.softmax(lb_logits.astype(F32), axis=0), axis=0)
    row = lambda a: a.reshape(1, -1).astype(F32)
    w = {
        "lb": row(lb_all[0]), "g_mix0": row(g_mix[0]), "g_mix1": row(g_mix[1]),
        "g_mlp0": row(g_mlp[0]), "g_mlp1": row(g_mlp[1]), "g_kv": row(g_kv),
        "g_final": row(g_final), "g_onorm": row(g_onorm_a[0]),
        "sb_bias": sb_bias[0].astype(F32),
        "w_in": w_in_a[0].astype(BF16), "w_out_a": w_out_a[0].astype(BF16),
        "w_kv": w_kv.astype(BF16), "w_kv_t": w_kv.T.astype(BF16), "w_q": w_q_b[0].astype(BF16),
        "w_out_b": w_out_b[0].astype(BF16),
        "w_up0": w_up[0].astype(BF16), "w_up1": w_up[1].astype(BF16),
        "w_down0": w_down[0].astype(BF16), "w_down1": w_down[1].astype(BF16),
    }
    n_pool = cache_k.shape[0]
    page_t = lambda a: jnp.transpose(a, (0, 2, 3, 1)).reshape(n_pool, D_MODEL, PAGE)
    past = (page_t(cache_k), page_t(cache_v), page_table)
    return _trunks(x_prompt, x_sample, state_hgrn[0], past, w)
```

```python
import functools
import math

import jax
import jax.numpy as jnp
from jax import lax
from jax.experimental import pallas as pl
from jax.experimental.pallas import tpu as pltpu

F32 = jnp.float32
BF16 = jnp.bfloat16

D_MODEL = 1024
D_FF = 4 * D_MODEL
H_A = 8
DK_A = 128
H_B = 16
HD_B = 64
EPS = 1e-6
PAGE = 128
LANES = 128
SUBLANES = 8
VMEM_LIMIT = 56 * 1024 * 1024
LOG2E = math.log2(math.e)
EXP2_CLAMP = 126.0

PRE_TM = 512
POST0_TM = 256
POST1_TM = 512
HGRN_PROMPT = dict(c=64, tb=128, hg=8, bb=1)
HGRN_SAMPLE_SEQS = 4
HGRN_STATIC_CHUNKS = 2
ATTN_TQ = 256
ATTN_STREAMS = 4


def _dot(a, b):
    return jnp.dot(a, b, preferred_element_type=F32)


def _dot_nt(a, b):
    return lax.dot_general(a, b, (((1,), (1,)), ((), ())), preferred_element_type=F32)


def _dot_tn(a, b):
    return lax.dot_general(a, b, (((0,), (0,)), ((), ())), preferred_element_type=F32)


def _inv_rms(x):
    return lax.rsqrt(jnp.mean(x * x, axis=-1, keepdims=True) + EPS)


def _silu(x):
    return x * (1.0 / (1.0 + jnp.exp(-x)))


def _resident(shape):
    nd = len(shape)
    return pl.BlockSpec(shape, lambda *_: (0,) * nd, pipeline_mode=pl.Buffered(1))


def _params(*sem):
    return pltpu.CompilerParams(dimension_semantics=sem, vmem_limit_bytes=VMEM_LIMIT)


def _pre_kernel(x_ref, g_ref, w_ref, o_ref):
    x = x_ref[...]
    xn = (x * _inv_rms(x) * g_ref[...]).astype(BF16)
    o_ref[...] = _dot(xn, w_ref[...])


def _pre(x, g, w, tm):
    t, d = x.shape
    n = w.shape[1]
    return pl.pallas_call(
        _pre_kernel,
        grid=(t // tm,),
        in_specs=[pl.BlockSpec((tm, d), lambda i: (i, 0)),
                  _resident((1, d)),
                  _resident((d, n))],
        out_specs=pl.BlockSpec((tm, n), lambda i: (i, 0)),
        out_shape=jax.ShapeDtypeStruct((t, n), F32),
        compiler_params=_params("parallel"),
        name="pre_proj",
    )(x, g, w)


def _level_ref_rows(g, m):
    c, w = g.shape
    if m >= SUBLANES:
        g3 = g.reshape(c // (2 * m), 2 * m, w)
        r = jnp.broadcast_to(g3[:, m - 1:m, :], g3.shape)
        return r.reshape(c, w)
    g3 = g.reshape(c // SUBLANES, SUBLANES, w)
    rib = lax.broadcasted_iota(jnp.int32, g3.shape, 1)
    out = None
    for start in range(0, SUBLANES, 2 * m):
        r = jnp.broadcast_to(g3[:, start + m - 1:start + m, :], g3.shape)
        out = r if out is None else jnp.where(rib >= start, r, out)
    return out.reshape(c, w)


def _split3(x):
    hi = x.astype(BF16)
    r1 = x - hi.astype(F32)
    mid = r1.astype(BF16)
    lo = (r1 - mid.astype(F32)).astype(BF16)
    return hi, mid, lo


def _hgrn_chunk(qr, fz, iv, og, lb, gon, s_prev):
    c, w = qr.shape
    hg = w // LANES
    head = lambda a, h: a[:, h * LANES:(h + 1) * LANES]
    row = lax.broadcasted_iota(jnp.int32, (c, c), 0)
    col = lax.broadcasted_iota(jnp.int32, (c, c), 1)
    xor = row ^ col
    lower = row > col

    qv = _silu(qr)
    e = jnp.exp(-jnp.abs(fz))
    r = 1.0 / (1.0 + e)
    er = e * r
    pos = fz >= 0.0
    sig = jnp.where(pos, r, er)
    nsig = jnp.where(pos, er, r)
    lf = jnp.log(lb + (1.0 - lb) * sig)
    kk = (1.0 - lb) * nsig

    tri = jnp.where(row >= col, 1.0, 0.0).astype(BF16)
    gcat = _dot(tri, jnp.concatenate(_split3(lf), axis=1))
    g = gcat[:, :w] + gcat[:, w:2 * w] + gcat[:, 2 * w:]

    vb = iv.astype(BF16)
    qg = (qv * jnp.exp(g)).astype(BF16)
    qb = qv.astype(BF16)
    kb = kk.astype(BF16)

    scores = [jnp.where(row == col, _dot_nt(head(qb, h), head(kb, h)), 0.0) for h in range(hg)]
    m = 1
    while m < c:
        em = jnp.exp(-jnp.abs(g - _level_ref_rows(g, m)))
        qe = (qv * em).astype(BF16)
        ke = (kk * em).astype(BF16)
        sel = lower & (xor >= m) & (xor < 2 * m)
        scores = [jnp.where(sel, _dot_nt(head(qe, h), head(ke, h)), scores[h])
                  for h in range(hg)]
        m *= 2
    o = [_dot(head(qg, h), s_prev[h].astype(BF16)) + _dot(scores[h].astype(BF16), head(vb, h))
         for h in range(hg)]

    g_last = g[c - 1:c, :]
    kdec = (kk * jnp.exp(g_last - g)).astype(BF16)
    dh, dm, dl = (p.astype(F32) for p in _split3(jnp.exp(g_last)))
    rid = lax.broadcasted_iota(jnp.int32, (SUBLANES, w), 0)
    d3 = jnp.where(rid == 0, dh, jnp.where(rid == 1, dm, jnp.where(rid == 2, dl, 0.0)))
    d3 = d3.astype(BF16)
    ones = jnp.ones((SUBLANES, LANES), BF16)
    s_new = [_dot_tn(head(d3, h), ones) * s_prev[h] + _dot_tn(head(kdec, h), head(vb, h))
             for h in range(hg)]

    on = jnp.concatenate([o[h] * _inv_rms(o[h]) for h in range(hg)], axis=1)
    return on * gon * _silu(og), s_new


def _hgrn_kernel(*refs, c, tb, hg, bb, has_s0, rider=None):
    if has_s0:
        q_ref, fz_ref, iv_ref, og_ref, lb_ref, gon_ref, s0_ref, o_ref, so_ref, s_scr = refs
    else:
        q_ref, fz_ref, iv_ref, og_ref, lb_ref, gon_ref, o_ref, so_ref, s_scr = refs
    ci = pl.program_id(2)
    w = hg * LANES

    @pl.when(ci == 0)
    def _init():
        if has_s0:
            s_scr[...] = s0_ref[...]
        else:
            s_scr[...] = jnp.zeros_like(s_scr)

    wide = lambda ref, rows: jnp.concatenate([ref[bi, rows, :] for bi in range(bb)], axis=1)
    lb = jnp.concatenate([lb_ref[...]] * bb, axis=1)
    gon = jnp.concatenate([gon_ref[...]] * bb, axis=1)

    def chunk(rows):
        on, s_new = _hgrn_chunk(wide(q_ref, rows), wide(fz_ref, rows), wide(iv_ref, rows),
                                wide(og_ref, rows), lb, gon,
                                [s_scr[bi, h] for bi in range(bb) for h in range(hg)])
        for bi in range(bb):
            o_ref[bi, rows, :] = on[:, bi * w:(bi + 1) * w].astype(o_ref.dtype)
            for h in range(hg):
                s_scr[bi, h] = s_new[bi * hg + h]

    if tb // c <= HGRN_STATIC_CHUNKS:
        for j in range(tb // c):
            chunk(slice(j * c, (j + 1) * c))
            if rider is not None and j == 0:
                rider()
    else:
        assert rider is None
        def body(j, carry):
            chunk(pl.ds(pl.multiple_of(j * c, c), c))
            return carry
        lax.fori_loop(0, tb // c, body, 0)

    @pl.when(ci == pl.num_programs(2) - 1)
    def _fin():
        so_ref[...] = s_scr[...]


def _hgrn_rider_kernel(pt_ref, *refs, n_in, n_rider_in, hgrn_kernel, rider_kernel):
    del pt_ref
    o_ref, so_ref, r_out, s_scr = refs[n_in + n_rider_in:]
    rider = functools.partial(rider_kernel, *refs[n_in:n_in + n_rider_in], r_out)
    hgrn_kernel(*refs[:n_in], o_ref, so_ref, s_scr, rider=rider)


def _hgrn(proj, lb, gon, s0, *, c, tb, hg, bb, out_dtype, make_rider=None):
    b, l, _ = proj.shape
    nhg = H_A // hg
    w = hg * LANES
    grid = (b // bb, nhg, l // tb)

    def tok_spec(k):
        return pl.BlockSpec((bb, tb, w), lambda bi, hi, ci, *_, k=k: (bi, ci, k * nhg + hi))

    vec_spec = pl.BlockSpec((1, w), lambda bi, hi, ci, *_: (0, hi))
    st_spec = pl.BlockSpec((bb, hg, DK_A, DK_A), lambda bi, hi, ci, *_: (bi, hi, 0, 0))
    in_specs = [tok_spec(0), tok_spec(1), tok_spec(2), tok_spec(3), vec_spec, vec_spec]
    args = [proj, proj, proj, proj, lb, gon]
    if s0 is not None:
        in_specs.append(st_spec)
        args.append(s0)
    body = functools.partial(_hgrn_kernel, c=c, tb=tb, hg=hg, bb=bb, has_s0=s0 is not None)
    out_specs = [pl.BlockSpec((bb, tb, w), lambda bi, hi, ci, *_: (bi, ci, hi)), st_spec]
    out_shape = [jax.ShapeDtypeStruct((b, l, D_MODEL), out_dtype),
                 jax.ShapeDtypeStruct((b, H_A, DK_A, DK_A), F32)]
    scratch = [pltpu.VMEM((bb, hg, DK_A, DK_A), F32)]
    params = _params("parallel", "parallel", "arbitrary")
    if make_rider is None:
        return pl.pallas_call(body, grid=grid, in_specs=in_specs, out_specs=out_specs,
                              out_shape=out_shape, scratch_shapes=scratch,
                              compiler_params=params, name="hgrn")(*args)
    rider = make_rider(grid)
    assert rider["steps"] == grid[0] * grid[1] * grid[2]
    grid_spec = pltpu.PrefetchScalarGridSpec(
        num_scalar_prefetch=1, grid=grid,
        in_specs=in_specs + rider["in_specs"],
        out_specs=out_specs + [rider["out_spec"]],
        scratch_shapes=scratch)
    return pl.pallas_call(
        functools.partial(_hgrn_rider_kernel, n_in=len(args), n_rider_in=len(rider["args"]),
                          hgrn_kernel=body, rider_kernel=rider["kernel"]),
        grid_spec=grid_spec,
        out_shape=out_shape + [rider["out_shape"]],
        compiler_params=params,
        name="hgrn_sample_attn",
    )(rider["prefetch"], *args, *rider["args"])


def _mix_mlp(x_ref, o_ref, wo_ref, gm_ref, wu_ref, wd_ref):
    h1 = x_ref[...] + _dot(o_ref[...].astype(BF16), wo_ref[...])
    xn = (h1 * _inv_rms(h1) * gm_ref[...]).astype(BF16)
    h2 = h1
    for f in range(D_FF // D_MODEL):
        cols = slice(f * D_MODEL, (f + 1) * D_MODEL)
        u = jnp.maximum(_dot(xn, wu_ref[:, cols]), 0.0)
        h2 = h2 + _dot((u * u).astype(BF16), wd_ref[cols, :])
    return h2


def _post0_kernel(x_ref, o_ref, wo_ref, gm_ref, wu_ref, wd_ref, gkv_ref, wkv_ref, gq_ref,
                  wq_ref, h_ref, k_ref, v_ref, q_ref, *, kv_t):
    h2 = _mix_mlp(x_ref, o_ref, wo_ref, gm_ref, wu_ref, wd_ref)
    h_ref[...] = h2
    hn = h2 * _inv_rms(h2)
    hkv = (hn * gkv_ref[...]).astype(BF16)
    if kv_t:
        kvt = _dot_nt(wkv_ref[...], hkv)
        k_ref[0] = kvt[:D_MODEL, :]
        v_ref[0] = kvt[D_MODEL:, :]
    else:
        kv = _dot(hkv, wkv_ref[...])
        k_ref[...] = kv[:, :D_MODEL]
        v_ref[...] = kv[:, D_MODEL:]
    q_ref[...] = _dot((hn * gq_ref[...]).astype(BF16), wq_ref[...]) * (HD_B ** -0.5 * LOG2E)


def _post1_kernel(x_ref, o_ref, wo_ref, gm_ref, wu_ref, wd_ref, gf_ref, y_ref):
    h2 = _mix_mlp(x_ref, o_ref, wo_ref, gm_ref, wu_ref, wd_ref)
    y_ref[...] = h2 * _inv_rms(h2) * gf_ref[...]


def _tok_spec(tm, n):
    return pl.BlockSpec((tm, n), lambda i: (i, 0))


def _post0(x, o, wo, gm, wu, wd, gkv, wkv, gq, wq, tm, seq_t=None):
    t, d = x.shape
    weights = [wo, gm, wu, wd, gkv, wkv, gq, wq]
    kv_spec, kv_shape = _tok_spec(tm, d), jax.ShapeDtypeStruct((t, d), F32)
    if seq_t is not None:
        b, l = seq_t
        nb = l // tm
        kv_spec = pl.BlockSpec((1, d, tm), lambda i: (i // nb, 0, i % nb))
        kv_shape = jax.ShapeDtypeStruct((b, d, l), F32)
    return pl.pallas_call(
        functools.partial(_post0_kernel, kv_t=seq_t is not None),
        grid=(t // tm,),
        in_specs=[_tok_spec(tm, d), _tok_spec(tm, d)] + [_resident(a.shape) for a in weights],
        out_specs=[_tok_spec(tm, d), kv_spec, kv_spec, _tok_spec(tm, d)],
        out_shape=[jax.ShapeDtypeStruct((t, d), F32), kv_shape, kv_shape,
                   jax.ShapeDtypeStruct((t, d), F32)],
        compiler_params=_params("parallel"),
        name="post0",
    )(x, o, *weights)


def _post1(x, o, wo, gm, wu, wd, gf, tm):
    t, d = x.shape
    weights = [wo, gm, wu, wd, gf]
    return pl.pallas_call(
        _post1_kernel,
        grid=(t // tm,),
        in_specs=[_tok_spec(tm, d), _tok_spec(tm, d)] + [_resident(a.shape) for a in weights],
        out_specs=_tok_spec(tm, d),
        out_shape=jax.ShapeDtypeStruct((t, d), F32),
        compiler_params=_params("parallel"),
        name="post1",
    )(x, o, *weights)


def _sb_group(ys, biases, carries, tt, masks):
    n = len(ys)
    u = ys[0].shape[1] // LANES
    carries = list(carries)
    parts = [[None] * u for _ in range(n)]

    def softplus2(y, mask):
        v = jnp.maximum(y, jnp.log2(1.0 + jnp.exp2(jnp.minimum(y, EXP2_CLAMP))))
        v = v if mask is None else jnp.where(mask, v, 0.0)
        return v.astype(BF16)

    for lo in reversed(range(0, u, 2)):
        blocks = [lo, lo + 1] if lo + 1 < u else [lo]
        y = [[ys[k][:, i * LANES:(i + 1) * LANES] + biases[k] for i in blocks] for k in range(n)]
        v = [[softplus2(y[k][b], None if masks is None else masks[i])
              for b, i in enumerate(blocks)] for k in range(n)]
        if len(blocks) == 1:
            v = [vk + [jnp.zeros_like(vk[0])] for vk in v]
        r = [_dot(jnp.concatenate(vk, axis=1), tt) for vk in v]
        for k in range(n):
            for b, i in enumerate(blocks):
                a = jnp.exp2(y[k][b] - r[k][:, b * LANES:(b + 1) * LANES] - carries[k])
                if masks is not None:
                    a = jnp.where(masks[i], a, 0.0)
                parts[k][i] = a.astype(BF16)
            carries[k] = carries[k] + jnp.broadcast_to(r[k][:, 0:1], carries[k].shape)
    return [jnp.concatenate(p, axis=1) for p in parts], carries


def _attn_kernel(bias_ref, q_ref, k_ref, v_ref, tt_ref, o_ref, kb, vb, acc, carry, *, tq, ns):
    hg = pl.program_id(1)
    qi = pl.program_id(2)
    nsub = tq // LANES
    assert nsub == 2
    nblk = kb.shape[0] // ns

    @pl.when(qi == 0)
    def _cast():
        for s in range(ns):
            feat = slice(s * LANES, (s + 1) * LANES)
            for j in range(nblk):
                keys = slice(j * LANES, (j + 1) * LANES)
                kb[s * nblk + j] = k_ref[0, feat, keys].astype(BF16)
                vb[s * nblk + j] = v_ref[0, feat, keys].astype(BF16)

    lane = lax.broadcasted_iota(jnp.int32, (tq, LANES), 1)
    first = lane < HD_B
    rows2 = lax.broadcasted_iota(jnp.int32, (2 * tq, LANES), 0)
    q2, bias2 = [], []
    for s in range(ns):
        q = q_ref[0, :, s * LANES:(s + 1) * LANES]
        q2.append(jnp.concatenate([jnp.where(first, q, 0.0), jnp.where(first, 0.0, q)],
                                  axis=0).astype(BF16))
        hp = hg * ns + s
        bias2.append(jnp.where(rows2 < tq, bias_ref[2 * hp], bias_ref[2 * hp + 1]) * LOG2E)
    tt = tt_ref[...]

    def group(j0, u, masks):
        blocks = lambda ref, s: jnp.concatenate([ref[s * nblk + j0 + i] for i in range(u)], axis=1)
        ys = [_dot(q2[s], blocks(kb, s)) for s in range(ns)]
        a, c_new = _sb_group(ys, bias2, [carry[s] for s in range(ns)], tt, masks)
        for s in range(ns):
            carry[s] = c_new[s]
        return [_dot_nt(a[s], blocks(vb, s)) for s in range(ns)]

    def accumulate(res):
        for s in range(ns):
            acc[s] += res[s]

    carry[...] = jnp.zeros_like(carry)
    t_idx = jnp.where(rows2 < tq, rows2, rows2 - tq)
    s_idx = lax.broadcasted_iota(jnp.int32, (2 * tq, LANES), 1)
    res = group(qi * nsub, nsub, [s_idx + i * LANES < t_idx for i in range(nsub)])
    for s in range(ns):
        acc[s] = res[s]

    @pl.when(qi % 2 == 1)
    def _pair():
        accumulate(group(qi * nsub - 2, 2, None))

    top = (qi // 2) * 4

    def body(g, c):
        accumulate(group(top - 4 * (g + 1), 4, None))
        return c
    lax.fori_loop(0, qi // 2, body, 0)

    o_ref[0] = jnp.concatenate(
        [jnp.where(first, acc[s, :tq, :], acc[s, tq:, :]) for s in range(ns)],
        axis=1).astype(o_ref.dtype)


def _tri_table():
    j = jnp.arange(LANES)[:, None]
    s = jnp.arange(LANES)[None, :]
    t = (j >= s).astype(BF16)
    top = jnp.concatenate([t, jnp.zeros((LANES, LANES), BF16)], axis=1)
    bottom = jnp.concatenate([jnp.ones((LANES, LANES), BF16), t], axis=1)
    return jnp.concatenate([top, bottom], axis=0)


def _attn_prompt(q, kt, vt, bias, tq, ns):
    b, l, d = q.shape
    w = ns * LANES
    nblk = l // LANES
    kv_spec = pl.BlockSpec((1, w, l), lambda bi, hg, qi: (bi, hg, 0))
    q_spec = pl.BlockSpec((1, tq, w), lambda bi, hg, qi: (bi, qi, hg))
    return pl.pallas_call(
        functools.partial(_attn_kernel, tq=tq, ns=ns),
        grid=(b, d // w, l // tq),
        in_specs=[pl.BlockSpec(memory_space=pltpu.SMEM), q_spec, kv_spec, kv_spec,
                  _resident((2 * LANES, 2 * LANES))],
        out_specs=q_spec,
        out_shape=jax.ShapeDtypeStruct((b, l, d), BF16),
        scratch_shapes=[pltpu.VMEM((ns * nblk, LANES, LANES), BF16),
                        pltpu.VMEM((ns * nblk, LANES, LANES), BF16),
                        pltpu.VMEM((ns, 2 * tq, LANES), F32),
                        pltpu.VMEM((ns, 2 * tq, LANES), F32)],
        compiler_params=_params("parallel", "parallel", "arbitrary"),
        name="attn_prompt",
    )(bias, q, kt, vt, _tri_table())


def _sattn_body(q_ref, kn_ref, vn_ref, *rest, lq, npp):
    kp_refs, vp_refs = rest[:npp], rest[npp:2 * npp]
    bias_ref, tt_ref, o_ref = rest[2 * npp:]
    rows = H_B * lq
    row = lax.broadcasted_iota(jnp.int32, (rows, D_MODEL), 0)
    col = lax.broadcasted_iota(jnp.int32, (rows, D_MODEL), 1)
    own = (col // HD_B) == (row // lq)
    tt = tt_ref[...]
    bias = bias_ref[...]

    q16 = jnp.concatenate([q_ref[0]] * H_B, axis=0)
    qbd = jnp.where(own, q16, 0.0).astype(BF16)
    pad = jnp.zeros((PAGE - lq, D_MODEL), F32)
    kblk = jnp.concatenate([kn_ref[0], pad], axis=0).astype(BF16)
    vblk = jnp.concatenate([vn_ref[0], pad], axis=0).astype(BF16)
    t_idx = lax.broadcasted_iota(jnp.int32, (rows, LANES), 0) % lq
    s_idx = lax.broadcasted_iota(jnp.int32, (rows, LANES), 1)
    (a,), (carry,) = _sb_group([_dot_nt(qbd, kblk)], [bias],
                               [jnp.zeros((rows, LANES), F32)], tt, [s_idx < t_idx])
    out_new = _dot(a, vblk)

    kcat = jnp.concatenate([kp_refs[i][0].astype(BF16) for i in reversed(range(npp))], axis=1)
    (a,), _ = _sb_group([_dot(qbd, kcat)], [bias], [carry], tt, None)
    vcat = jnp.concatenate([vp_refs[i][0].astype(BF16) for i in reversed(range(npp))], axis=1)
    out_past = _dot_nt(a, vcat)

    sel = jnp.where(own, out_new + out_past, 0.0).reshape(H_B, lq, D_MODEL)
    o_ref[0] = jnp.sum(sel, axis=0)


def _sample_attn_rider(q, kn, vn, cache_kt, cache_vt, page_table, bias, seq_of):
    b, lq, d = q.shape
    n_pages = page_table.shape[1]
    rows = H_B * lq
    bias_rows = jnp.broadcast_to(jnp.repeat(bias * LOG2E, lq)[:, None], (rows, LANES))

    def page_spec(i):
        return pl.BlockSpec((1, d, PAGE),
                            lambda *ids: (ids[-1][seq_of(*ids[:-1]), n_pages - 1 - i], 0, 0))

    seq_spec = pl.BlockSpec((1, lq, d), lambda *ids: (seq_of(*ids[:-1]), 0, 0))
    pages = [page_spec(i) for i in range(n_pages)]
    return dict(
        kernel=functools.partial(_sattn_body, lq=lq, npp=n_pages),
        args=[q, kn, vn] + [cache_kt] * n_pages + [cache_vt] * n_pages + [bias_rows, _tri_table()],
        in_specs=[seq_spec, seq_spec, seq_spec] + pages + pages + [
            pl.BlockSpec((rows, LANES), lambda *ids: (0, 0)),
            pl.BlockSpec((2 * LANES, 2 * LANES), lambda *ids: (0, 0))],
        out_spec=seq_spec,
        out_shape=jax.ShapeDtypeStruct((b, lq, d), F32),
        prefetch=page_table,
        steps=b,
    )


def _post0_args(xt, o, w):
    return (xt, o, w["w_out_a"], w["g_mlp0"], w["w_up0"], w["w_down0"], w["g_kv"])


def _post1_call(h, att, w):
    return _post1(h, att, w["w_out_b"], w["g_mlp1"], w["w_up1"], w["w_down1"], w["g_final"],
                  POST1_TM)


def _trunks(x_p, x_s, s0_s, past, w):
    cache_kt, cache_vt, page_table = past
    d = D_MODEL

    bs, ls, _ = x_s.shape
    xs = x_s.reshape(bs * ls, d)
    proj_s = _pre(xs, w["g_mix0"], w["w_in"], PRE_TM)
    o_s, st_s = _hgrn(proj_s.reshape(bs, ls, 4 * d), w["lb"], w["g_onorm"], s0_s,
                      c=ls, tb=ls, hg=H_A, bb=HGRN_SAMPLE_SEQS, out_dtype=F32)
    h_s, k_s, v_s, q_s = _post0(*_post0_args(xs, o_s.reshape(bs * ls, d), w), w["w_kv"],
                                w["g_mix1"], w["w_q"], POST0_TM)
    k3, v3, q3 = (a.reshape(bs, ls, d) for a in (k_s, v_s, q_s))

    def make_rider(grid):
        seq_of = lambda bi, hi, ci: (bi * grid[1] + hi) * grid[2] + ci
        return _sample_attn_rider(q3, k3, v3, cache_kt, cache_vt, page_table, w["sb_bias"],
                                  seq_of)

    bp, lp, _ = x_p.shape
    xp = x_p.reshape(bp * lp, d)
    proj_p = _pre(xp, w["g_mix0"], w["w_in"], PRE_TM)
    o_p, st_p, att_s = _hgrn(proj_p.reshape(bp, lp, 4 * d), w["lb"], w["g_onorm"], None,
                             out_dtype=BF16, make_rider=make_rider, **HGRN_PROMPT)
    h_p, kt, vt, q_p = _post0(*_post0_args(xp, o_p.reshape(bp * lp, d), w), w["w_kv_t"],
                              w["g_mix1"], w["w_q"], POST0_TM, seq_t=(bp, lp))
    att_p = _attn_prompt(q_p.reshape(bp, lp, d), kt, vt, w["sb_bias"], ATTN_TQ, ATTN_STREAMS)
    k_p, v_p = (jnp.transpose(a.reshape(bp, H_B, HD_B, lp), (0, 3, 1, 2)) for a in (kt, vt))

    y_p = _post1_call(h_p, att_p.reshape(bp * lp, d), w).reshape(bp, lp, d)
    y_s = _post1_call(h_s, att_s.reshape(bs * ls, d), w).reshape(bs, ls, d)
    return (y_p, y_s, st_p[None], st_s[None], k_p, v_p,
            k_s.reshape(bs, ls, H_B, HD_B), v_s.reshape(bs, ls, H_B, HD_B))


def kernel(x_prompt, x_sample, state_hgrn, cache_k, cache_v, page_table, lb_logits, w_in_a,
           g_onorm_a, w_out_a, w_kv, g_kv, w_q_b, w_out_b, sb_bias, w_up, w_down, g_mix,
           g_mlp, g_final):
    lb_all = jnp.cumsum(jax.nn.softmax(lb_logits.astype(F32), axis=0), axis=0)
    row = lambda a: a.reshape(1, -1).astype(F32)
    w = {
        "lb": row(lb_all[0]), "g_mix0": row(g_mix[0]), "g_mix1": row(g_mix[1]),
        "g_mlp0": row(g_mlp[0]), "g_mlp1": row(g_mlp[1]), "g_kv": row(g_kv),
        "g_final": row(g_final), "g_onorm": row(g_onorm_a[0]),
        "sb_bias": sb_bias[0].astype(F32),
        "w_in": w_in_a[0].astype(BF16), "w_out_a": w_out_a[0].astype(BF16),
        "w_kv": w_kv.astype(BF16), "w_kv_t": w_kv.T.astype(BF16), "w_q": w_q_b[0].astype(BF16),
        "w_out_b": w_out_b[0].astype(BF16),
        "w_up0": w_up[0].astype(BF16), "w_up1": w_up[1].astype(BF16),
        "w_down0": w_down[0].astype(BF16), "w_down1": w_down[1].astype(BF16),
    }
    n_pool = cache_k.shape[0]
    page_t = lambda a: jnp.transpose(a, (0, 2, 3, 1)).reshape(n_pool, D_MODEL, PAGE)
    past = (page_t(cache_k), page_t(cache_v), page_table)
    return _trunks(x_prompt, x_sample, state_hgrn[0], past, w)
```
